```python
import jax
import jax.numpy as jnp
from jax import lax
import numpy as np


D_MODEL = 1024
BATCH = 8
SEQ = 4096
DEPTH = 4

N_MIXERS = 2
EPS = 1e-6
HG_EXPAND = 128
HG_HEADS = D_MODEL // HG_EXPAND
HG_DK = HG_EXPAND
HG_DV = D_MODEL // HG_HEADS
HG_CHUNK = 64
MB_HEADS = 8
MB_HEAD_DIM = D_MODEL // MB_HEADS
MB_BLOCK = 256
MB_TOPK = 3
MB_QCHUNK = 16
ROPE_THETA = 10000.0
FFN_DIM = 3 * D_MODEL
CONV_WIDTH = 3
N_HGRN_LAYERS = len(range(0, DEPTH, N_MIXERS))
N_MOBA_LAYERS = DEPTH - N_HGRN_LAYERS

kernel_name = 'hgrn2_moba_convglu_hybrid'


def rms_norm(x, gain):
    xf = x.astype(jnp.float32)
    y = xf * lax.rsqrt(jnp.mean(xf * xf, axis=-1, keepdims=True) + EPS) * gain
    return y.astype(x.dtype)


def rope_tables(seq):
    inv = 1.0 / (ROPE_THETA ** (jnp.arange(0, MB_HEAD_DIM, 2, dtype=jnp.float32) / MB_HEAD_DIM))
    ang = jnp.arange(seq, dtype=jnp.float32)[:, None] * inv[None, :]
    ang = jnp.concatenate([ang, ang], axis=-1)
    return jnp.cos(ang), jnp.sin(ang)


def apply_rope(t, cos, sin):
    t1, t2 = jnp.split(t, 2, axis=-1)
    return t * cos + jnp.concatenate([-t2, t1], axis=-1) * sin


def hgrn2_mixer(h, w_in, lb, out_gain, w_out):
    bsz, seq, _ = h.shape
    nc = seq // HG_CHUNK
    hk, hv = HG_HEADS * HG_DK, HG_HEADS * HG_DV

    def heads(t, d):
        t = t.reshape(bsz, seq, HG_HEADS, d).transpose(0, 2, 1, 3)
        return t.reshape(bsz, HG_HEADS, nc, HG_CHUNK, d)

    q, fz, v, g = jnp.split(h @ w_in, [hk, 2 * hk, 2 * hk + hv], axis=-1)
    q = heads(q, HG_DK).astype(jnp.float32) * HG_DK ** -0.5
    fz = heads(fz, HG_DK).astype(jnp.float32)
    v = heads(v, HG_DV)
    lb = lb.astype(jnp.float32).reshape(1, HG_HEADS, 1, 1, HG_DK)
    log_f = jnp.logaddexp(jnp.log(lb), jnp.log1p(-lb) + jax.nn.log_sigmoid(fz))
    k = (1.0 - lb) * jax.nn.sigmoid(-fz)
    G = jnp.cumsum(log_f, axis=3)

    g_mid = G[:, :, :, HG_CHUNK // 2 - 1:HG_CHUNK // 2]
    q_rel = q * jnp.exp(G - g_mid)
    k_rel = k * jnp.exp(g_mid - G)
    causal = jnp.tril(jnp.ones((HG_CHUNK, HG_CHUNK), dtype=bool))
    att = jnp.where(causal, jnp.einsum('bhncd,bhnsd->bhncs', q_rel, k_rel), 0.0)
    o_intra = jnp.einsum('bhncs,bhnsv->bhncv', att, v)

    g_last = G[:, :, :, -1]
    q_in = q * jnp.exp(G)
    k_st = k * jnp.exp(g_last[:, :, :, None, :] - G)
    decay = jnp.exp(g_last)
    xs = tuple(jnp.moveaxis(t, 2, 0) for t in (q_in, k_st, v, decay))

    def step(state, inp):
        qi, ks, vc, dc = inp
        o = jnp.einsum('bhcd,bhdv->bhcv', qi, state)
        state = (dc[..., None] * state + jnp.einsum('bhcd,bhcv->bhdv', ks, vc)).astype(jnp.float32)
        return state, o

    s0 = jnp.zeros((bsz, HG_HEADS, HG_DK, HG_DV), jnp.float32)
    _, o_inter = lax.scan(step, s0, xs)
    o = (o_intra + jnp.moveaxis(o_inter, 0, 2)).reshape(bsz, HG_HEADS, seq, HG_DV)

    g = g.reshape(bsz, seq, HG_HEADS, HG_DV).transpose(0, 2, 1, 3).astype(jnp.float32)
    o = rms_norm(o, out_gain) * jax.nn.silu(g)
    o = o.transpose(0, 2, 1, 3).reshape(bsz, seq, hv).astype(h.dtype)
    return o @ w_out


def moba_mixer(h, w_qkv, q_gain, k_gain, w_out, cos, sin):
    bsz, seq, _ = h.shape
    qkv = (h @ w_qkv).reshape(bsz, seq, 3, MB_HEADS, MB_HEAD_DIM).transpose(2, 0, 3, 1, 4)
    q, k, v = qkv[0], qkv[1], qkv[2]
    q = apply_rope(rms_norm(q, q_gain), cos, sin) * MB_HEAD_DIM ** -0.5
    k = apply_rope(rms_norm(k, k_gain), cos, sin)

    n_blk = -(-seq // MB_BLOCK)
    pad = n_blk * MB_BLOCK - seq
    kb = jnp.pad(k, ((0, 0), (0, 0), (0, pad), (0, 0))).reshape(bsz, MB_HEADS, n_blk, MB_BLOCK, MB_HEAD_DIM)
    vb = jnp.pad(v, ((0, 0), (0, 0), (0, pad), (0, 0))).reshape(bsz, MB_HEADS, n_blk, MB_BLOCK, MB_HEAD_DIM)

    k_mean = jnp.mean(kb, axis=3)
    q_blk = jnp.arange(seq) // MB_BLOCK
    past = jnp.arange(n_blk)[None, :] < q_blk[:, None]
    gate = jnp.where(past, jnp.einsum('bhsd,bhnd->bhsn', q, k_mean), -jnp.inf)
    top = min(MB_TOPK, n_blk)
    _, sel = lax.top_k(gate, top)

    nq = seq // MB_QCHUNK
    q_c = q.reshape(bsz, MB_HEADS, nq, MB_QCHUNK, MB_HEAD_DIM).transpose(2, 0, 1, 3, 4)
    sel_c = sel.reshape(bsz, MB_HEADS, nq, MB_QCHUNK, top).transpose(2, 0, 1, 3, 4)
    b_idx = jnp.arange(bsz)[:, None, None, None]
    h_idx = jnp.arange(MB_HEADS)[None, :, None, None]

    def attend_chunk(args):
        qc, selc, c = args
        blk = (c * MB_QCHUNK) // MB_BLOCK
        k_sel = kb[b_idx, h_idx, selc]
        v_sel = vb[b_idx, h_idx, selc]
        k_own = lax.dynamic_index_in_dim(kb, blk, axis=2, keepdims=False)
        v_own = lax.dynamic_index_in_dim(vb, blk, axis=2, keepdims=False)
        ok_sel = (jnp.arange(top) < blk)[:, None]
        s_sel = jnp.where(ok_sel, jnp.einsum('bhqd,bhqnkd->bhqnk', qc, k_sel), -jnp.inf)
        s_sel = s_sel.reshape(bsz, MB_HEADS, MB_QCHUNK, top * MB_BLOCK)
        q_pos = c * MB_QCHUNK + jnp.arange(MB_QCHUNK)
        k_pos = blk * MB_BLOCK + jnp.arange(MB_BLOCK)
        s_own = jnp.where(k_pos[None, :] <= q_pos[:, None],
                          jnp.einsum('bhqd,bhkd->bhqk', qc, k_own), -jnp.inf)
        p = jax.nn.softmax(jnp.concatenate([s_sel, s_own], axis=-1).astype(jnp.float32), axis=-1)
        p_sel = p[..., :top * MB_BLOCK].reshape(bsz, MB_HEADS, MB_QCHUNK, top, MB_BLOCK)
        p_own = p[..., top * MB_BLOCK:]
        return (jnp.einsum('bhqnk,bhqnkd->bhqd', p_sel, v_sel)
                + jnp.einsum('bhqk,bhkd->bhqd', p_own, v_own))

    o = lax.map(attend_chunk, (q_c, sel_c, jnp.arange(nq)))
    o = o.transpose(1, 0, 3, 2, 4).reshape(bsz, seq, MB_HEADS * MB_HEAD_DIM).astype(h.dtype)
    return o @ w_out


def conv_glu_ffn(h, w_up, conv_w, conv_b, w_down):
    seq = h.shape[1]
    a, u = jnp.split(h @ w_up, 2, axis=-1)
    a_pad = jnp.pad(a, ((0, 0), (CONV_WIDTH - 1, 0), (0, 0)))
    a = sum(conv_w[j] * a_pad[:, j:j + seq] for j in range(CONV_WIDTH)) + conv_b
    return (jax.nn.silu(a) * u) @ w_down


def setup_inputs(seed: int = 0) -> dict:
    key = jax.random.key(seed)
    ks = jax.random.split(key, 16)

    def nrm(k, shape, scale):
        return jax.random.normal(k, shape, jnp.float32) * scale

    hk, hv = HG_HEADS * HG_DK, HG_HEADS * HG_DV
    mb_w = MB_HEADS * MB_HEAD_DIM
    return {
        'x': nrm(ks[0], (BATCH, SEQ, D_MODEL), 1.0),
        'attn_norm': 1.0 + nrm(ks[1], (DEPTH, D_MODEL), 0.02),
        'ffn_norm': 1.0 + nrm(ks[2], (DEPTH, D_MODEL), 0.02),
        'hgrn_w_in': nrm(ks[3], (N_HGRN_LAYERS, D_MODEL, 2 * hk + 2 * hv), D_MODEL ** -0.5),
        'hgrn_lb': nrm(ks[4], (N_HGRN_LAYERS, hk), 0.1),
        'hgrn_out_norm': 1.0 + nrm(ks[5], (N_HGRN_LAYERS, HG_DV), 0.02),
        'hgrn_w_out': nrm(ks[6], (N_HGRN_LAYERS, hv, D_MODEL), hv ** -0.5),
        'moba_w_qkv': nrm(ks[7], (N_MOBA_LAYERS, D_MODEL, 3 * mb_w), D_MODEL ** -0.5),
        'moba_q_norm': 1.0 + nrm(ks[8], (N_MOBA_LAYERS, MB_HEAD_DIM), 0.02),
        'moba_k_norm': 1.0 + nrm(ks[9], (N_MOBA_LAYERS, MB_HEAD_DIM), 0.02),
        'moba_w_out': nrm(ks[10], (N_MOBA_LAYERS, mb_w, D_MODEL), mb_w ** -0.5),
        'ffn_w_up': nrm(ks[11], (DEPTH, D_MODEL, 2 * FFN_DIM), D_MODEL ** -0.5),
        'ffn_conv_w': nrm(ks[12], (DEPTH, CONV_WIDTH, FFN_DIM), CONV_WIDTH ** -0.5),
        'ffn_conv_b': nrm(ks[13], (DEPTH, FFN_DIM), 0.01),
        'ffn_w_down': nrm(ks[14], (DEPTH, FFN_DIM, D_MODEL), FFN_DIM ** -0.5),
    }


def reference(x, attn_norm, ffn_norm, hgrn_w_in, hgrn_lb, hgrn_out_norm, hgrn_w_out,
              moba_w_qkv, moba_q_norm, moba_k_norm, moba_w_out,
              ffn_w_up, ffn_conv_w, ffn_conv_b, ffn_w_down):
    seq = x.shape[1]
    cos, sin = rope_tables(seq)
    lb_cum = jnp.cumsum(jax.nn.softmax(hgrn_lb.astype(jnp.float32), axis=0), axis=0)
    lower_bounds = lb_cum - lb_cum[0]
    for layer in range(DEPTH):
        slot = layer // N_MIXERS
        h = rms_norm(x, attn_norm[layer])
        if layer % N_MIXERS == 0:
            mix = hgrn2_mixer(h, hgrn_w_in[slot], lower_bounds[slot], hgrn_out_norm[slot], hgrn_w_out[slot])
        else:
            mix = moba_mixer(h, moba_w_qkv[slot], moba_q_norm[slot], moba_k_norm[slot], moba_w_out[slot], cos, sin)
        x = x + mix.astype(x.dtype)
        ffn = conv_glu_ffn(rms_norm(x, ffn_norm[layer]), ffn_w_up[layer], ffn_conv_w[layer],
                           ffn_conv_b[layer], ffn_w_down[layer])
        x = x + ffn.astype(x.dtype)
    return x
```

```python
import functools

import jax
import jax.numpy as jnp
from jax import lax
from jax.experimental import pallas as pl
from jax.experimental.pallas import tpu as pltpu

EPS = 1e-6
HEAD_DIM = 128
HG_CHUNK = 64
MB_BLOCK = 256
MB_TOPK = 3
ROPE_THETA = 10000.0
CONV_WIDTH = 3
CARRY_ROWS = 8

VMEM_LIMIT_BYTES = 56 * 1024 * 1024

F32 = jnp.float32
BF16 = jnp.bfloat16


def _dot(a, b):
    return jnp.dot(a, b, preferred_element_type=F32)


def _dot_nt(a, b):
    return lax.dot_general(a, b, (((1,), (1,)), ((), ())), preferred_element_type=F32)


def _dot_tn(a, b):
    return lax.dot_general(a, b, (((0,), (0,)), ((), ())), preferred_element_type=F32)


def _rms_norm(x, gain):
    return x * lax.rsqrt(jnp.mean(x * x, axis=-1, keepdims=True) + EPS) * gain


def _split3_bf16(x):
    hi = x.astype(BF16)
    r1 = x - hi.astype(F32)
    mid = r1.astype(BF16)
    lo = (r1 - mid.astype(F32)).astype(BF16)
    return hi, mid, lo


def _params(n_grid_axes):
    return pltpu.CompilerParams(
        dimension_semantics=("arbitrary",) * n_grid_axes,
        vmem_limit_bytes=VMEM_LIMIT_BYTES,
    )


def _hgrn_kernel(x_ref, gain_ref, win_ref, lb_ref, og_ref, o_ref, proj_ref, state_ref, *,
                 slot, n_heads):
    tile, d_model = x_ref.shape
    n_chunks = tile // HG_CHUNK

    @pl.when(pl.program_id(1) == 0)
    def _():
        state_ref[...] = jnp.zeros_like(state_ref)

    h = _rms_norm(x_ref[...], gain_ref[...]).astype(BF16)
    proj_ref[...] = _dot(h, win_ref[...])

    logits = lb_ref[...]
    rows = [logits[i:i + 1, :] for i in range(logits.shape[0])]
    top = functools.reduce(jnp.maximum, rows)
    exps = [jnp.exp(r - top) for r in rows]
    denom = functools.reduce(jnp.add, exps)
    cum = []
    for e in exps:
        p = e / denom
        cum.append(p if not cum else cum[-1] + p)
    lb = cum[slot] - cum[0]
    log_lb = jnp.log(lb)
    log_1m_lb = jnp.log1p(-lb)
    one_m_lb = 1.0 - lb

    r_i = lax.broadcasted_iota(jnp.int32, (HG_CHUNK, HG_CHUNK), 0)
    c_i = lax.broadcasted_iota(jnp.int32, (HG_CHUNK, HG_CHUNK), 1)
    causal = r_i >= c_i
    tril = causal.astype(BF16)
    og = og_ref[...]
    mid = HG_CHUNK // 2 - 1

    def chunk_body(c, carry):
        rows_c = pl.ds(pl.multiple_of(c * HG_CHUNK, HG_CHUNK), HG_CHUNK)
        q = proj_ref[rows_c, 0:d_model] * (HEAD_DIM ** -0.5)
        fz = proj_ref[rows_c, d_model:2 * d_model]
        v = proj_ref[rows_c, 2 * d_model:3 * d_model].astype(BF16)
        g = proj_ref[rows_c, 3 * d_model:4 * d_model]

        e_neg = jnp.exp(-jnp.abs(fz))
        log_sig = jnp.minimum(fz, 0.0) - jnp.log1p(e_neg)
        b = log_1m_lb + log_sig
        top_ab = jnp.maximum(log_lb, b)
        log_f = top_ab + jnp.log1p(jnp.exp(-jnp.abs(log_lb - b)))
        inv = 1.0 / (1.0 + e_neg)
        k = one_m_lb * jnp.where(fz >= 0.0, e_neg * inv, inv)

        hi, md, lo = _split3_bf16(log_f)
        G = _dot(tril, hi) + _dot(tril, md) + _dot(tril, lo)
        g_mid = G[mid:mid + 1, :]
        g_last = G[HG_CHUNK - 1:HG_CHUNK, :]
        q_rel = (q * jnp.exp(G - g_mid)).astype(BF16)
        k_rel = (k * jnp.exp(g_mid - G)).astype(BF16)
        q_in = (q * jnp.exp(G)).astype(BF16)
        k_st = (k * jnp.exp(g_last - G)).astype(BF16)
        decay = jnp.exp(g_last)
        gate = g * jax.nn.sigmoid(g)

        for hd in range(n_heads):
            sl = slice(hd * HEAD_DIM, (hd + 1) * HEAD_DIM)
            att = jnp.where(causal, _dot_nt(q_rel[:, sl], k_rel[:, sl]), 0.0).astype(BF16)
            o = _dot(att, v[:, sl])
            st = state_ref[hd]
            o = o + _dot_nt(q_in[:, sl], st.astype(BF16))
            state_ref[hd] = st * decay[:, sl] + _dot_tn(v[:, sl], k_st[:, sl])
            y = _rms_norm(o, og) * gate[:, sl]
            o_ref[rows_c, sl] = y.astype(o_ref.dtype)
        return carry

    lax.fori_loop(0, n_chunks, chunk_body, 0)


def _hgrn_mixer(x, gain, w_in, lb_logits, out_gain, *, slot, tile):
    bsz, seq, d_model = x.shape
    n_heads = d_model // HEAD_DIM
    n_layers = lb_logits.shape[0]
    kern = functools.partial(_hgrn_kernel, slot=slot, n_heads=n_heads)
    return pl.pallas_call(
        kern,
        grid=(bsz, seq // tile),
        in_specs=[
            pl.BlockSpec((None, tile, d_model), lambda b, t: (b, t, 0)),
            pl.BlockSpec((1, d_model), lambda b, t: (0, 0)),
            pl.BlockSpec((d_model, 4 * d_model), lambda b, t: (0, 0)),
            pl.BlockSpec((n_layers, d_model), lambda b, t: (0, 0)),
            pl.BlockSpec((1, HEAD_DIM), lambda b, t: (0, 0)),
        ],
        out_specs=pl.BlockSpec((None, tile, d_model), lambda b, t: (b, t, 0)),
        out_shape=jax.ShapeDtypeStruct((bsz, seq, d_model), BF16),
        scratch_shapes=[
            pltpu.VMEM((tile, 4 * d_model), F32),
            pltpu.VMEM((n_heads, HEAD_DIM, HEAD_DIM), F32),
        ],
        compiler_params=_params(2),
        name="hgrn_mixer",
    )(x, gain.reshape(1, d_model), w_in, lb_logits, out_gain.reshape(1, HEAD_DIM))


def _qkv_kernel(x_ref, gain_ref, w_ref, qg_ref, kg_ref, cos_ref, sin_ref,
                q_ref, k_ref, v_ref, km_ref, *, n_heads):
    tile, d_model = x_ref.shape
    h = _rms_norm(x_ref[...], gain_ref[...]).astype(BF16)
    qkv = _dot(h, w_ref[...])
    cos = cos_ref[...]
    sin_signed = sin_ref[...]
    v_ref[...] = qkv[:, 2 * d_model:3 * d_model].astype(v_ref.dtype)
    for hd in range(n_heads):
        sl = slice(hd * HEAD_DIM, (hd + 1) * HEAD_DIM)
        qh = _rms_norm(qkv[:, hd * HEAD_DIM:(hd + 1) * HEAD_DIM], qg_ref[...])
        qh = (qh * cos + pltpu.roll(qh, HEAD_DIM // 2, axis=1) * sin_signed) * (HEAD_DIM ** -0.5)
        q_ref[:, sl] = qh.astype(q_ref.dtype)
        kh = _rms_norm(qkv[:, d_model + hd * HEAD_DIM:d_model + (hd + 1) * HEAD_DIM], kg_ref[...])
        kh = kh * cos + pltpu.roll(kh, HEAD_DIM // 2, axis=1) * sin_signed
        k_ref[:, sl] = kh.astype(k_ref.dtype)
        for j in range(tile // MB_BLOCK):
            km_ref[j, :, sl] = jnp.mean(kh[j * MB_BLOCK:(j + 1) * MB_BLOCK, :], axis=0, keepdims=True)


def _moba_qkv(x, gain, w_qkv, q_gain, k_gain, cos, sin_signed, *, tile):
    bsz, seq, d_model = x.shape
    n_heads = d_model // HEAD_DIM
    tok = pl.BlockSpec((None, tile, d_model), lambda b, t: (b, t, 0))
    rope = pl.BlockSpec((tile, HEAD_DIM), lambda b, t: (t, 0))
    vec = pl.BlockSpec((1, HEAD_DIM), lambda b, t: (0, 0))
    act = jax.ShapeDtypeStruct((bsz, seq, d_model), BF16)
    return pl.pallas_call(
        functools.partial(_qkv_kernel, n_heads=n_heads),
        grid=(bsz, seq // tile),
        in_specs=[
            tok,
            pl.BlockSpec((1, d_model), lambda b, t: (0, 0)),
            pl.BlockSpec((d_model, 3 * d_model), lambda b, t: (0, 0)),
            vec, vec, rope, rope,
        ],
        out_specs=[
            tok, tok, tok,
            pl.BlockSpec((None, tile // MB_BLOCK, 1, d_model), lambda b, t: (b, t, 0, 0)),
        ],
        out_shape=[act, act, act,
                   jax.ShapeDtypeStruct((bsz, seq // MB_BLOCK, 1, d_model), F32)],
        compiler_params=_params(2),
        name="moba_qkv",
    )(x, gain.reshape(1, d_model), w_qkv, q_gain.reshape(1, HEAD_DIM), k_gain.reshape(1, HEAD_DIM),
      cos, sin_signed)


def _moba_attn_kernel(q_ref, k_ref, v_ref, km_ref, o_ref):
    qb = pl.program_id(2)
    q = q_ref[...]
    n_cols = km_ref.shape[0]
    neg_inf = -jnp.inf

    km = km_ref[...]
    km_hi = km.astype(BF16)
    km_lo = (km - km_hi.astype(F32)).astype(BF16)
    gate = _dot_nt(q, km_hi) + _dot_nt(q, km_lo)
    col = lax.broadcasted_iota(jnp.int32, gate.shape, 1)
    gate = jnp.where(col < qb, gate, neg_inf)
    sel = jnp.zeros(gate.shape, F32)
    for _ in range(MB_TOPK):
        top = jnp.max(gate, axis=-1, keepdims=True)
        idx = jnp.min(jnp.where(gate == top, col, n_cols), axis=-1, keepdims=True)
        pick = (col == idx) & (top > neg_inf)
        sel = jnp.where(pick, 1.0, sel)
        gate = jnp.where(pick, neg_inf, gate)

    own = pl.ds(pl.multiple_of(qb * MB_BLOCK, MB_BLOCK), MB_BLOCK)
    s = _dot_nt(q, k_ref[own, :])
    r_i = lax.broadcasted_iota(jnp.int32, s.shape, 0)
    c_i = lax.broadcasted_iota(jnp.int32, s.shape, 1)
    s = jnp.where(c_i <= r_i, s, neg_inf)
    m0 = jnp.max(s, axis=-1, keepdims=True)
    p = jnp.exp(s - m0)
    l0 = jnp.sum(p, axis=-1, keepdims=True)
    acc0 = _dot(p.astype(BF16), v_ref[own, :])

    def body(j, carry):
        m, l, acc = carry
        blk = pl.ds(pl.multiple_of(j * MB_BLOCK, MB_BLOCK), MB_BLOCK)
        chosen = jnp.max(jnp.where(col == j, sel, 0.0), axis=-1, keepdims=True) > 0.5
        s = jnp.where(chosen, _dot_nt(q, k_ref[blk, :]), neg_inf)
        m_new = jnp.maximum(m, jnp.max(s, axis=-1, keepdims=True))
        alpha = jnp.exp(m - m_new)
        p = jnp.exp(s - m_new)
        l = alpha * l + jnp.sum(p, axis=-1, keepdims=True)
        acc = alpha * acc + _dot(p.astype(BF16), v_ref[blk, :])
        return m_new, l, acc

    _, l, acc = lax.fori_loop(0, qb, body, (m0, l0, acc0))
    o_ref[...] = (acc / l).astype(o_ref.dtype)


def _moba_attention(q, k, v, k_mean):
    bsz, seq, d_model = q.shape
    n_heads = d_model // HEAD_DIM
    n_cols = k_mean.shape[2]
    q_spec = pl.BlockSpec((None, MB_BLOCK, HEAD_DIM), lambda b, h, i: (b, i, h))
    kv_spec = pl.BlockSpec((None, seq, HEAD_DIM), lambda b, h, i: (b, 0, h))
    return pl.pallas_call(
        _moba_attn_kernel,
        grid=(bsz, n_heads, seq // MB_BLOCK),
        in_specs=[
            q_spec, kv_spec, kv_spec,
            pl.BlockSpec((None, None, n_cols, HEAD_DIM), lambda b, h, i: (b, h, 0, 0)),
        ],
        out_specs=q_spec,
        out_shape=jax.ShapeDtypeStruct((bsz, seq, d_model), BF16),
        compiler_params=_params(3),
        name="moba_attention",
    )(q, k, v, k_mean)


def _ffn_kernel(x_ref, o_ref, wout_ref, gain_ref, wup_ref, cw_ref, cb_ref, wdn_ref,
                y_ref, carry_ref, acc_ref, *, f_chunk):
    tile, d_model = x_ref.shape
    ffn_dim = wdn_ref.shape[0]

    @pl.when(pl.program_id(1) == 0)
    def _():
        carry_ref[...] = jnp.zeros_like(carry_ref)

    x1 = x_ref[...] + _dot(o_ref[...], wout_ref[...])
    h = _rms_norm(x1, gain_ref[...]).astype(BF16)
    row = lax.broadcasted_iota(jnp.int32, (tile, f_chunk), 0)
    for fc in range(ffn_dim // f_chunk):
        cs = slice(fc * f_chunk, (fc + 1) * f_chunk)
        a = _dot(h, wup_ref[:, cs])
        u = _dot(h, wup_ref[:, ffn_dim + fc * f_chunk:ffn_dim + (fc + 1) * f_chunk])
        prev = carry_ref[:, cs]
        p1 = prev[CARRY_ROWS - 1:CARRY_ROWS, :]
        p2 = prev[CARRY_ROWS - 2:CARRY_ROWS - 1, :]
        a1 = jnp.where(row == 0, p1, pltpu.roll(a, 1, axis=0))
        a2 = jnp.where(row == 0, p2, jnp.where(row == 1, p1, pltpu.roll(a, 2, axis=0)))
        carry_ref[:, cs] = a[tile - CARRY_ROWS:tile, :]
        conv = cw_ref[0:1, cs] * a2 + cw_ref[1:2, cs] * a1 + cw_ref[2:3, cs] * a + cb_ref[:, cs]
        act = (conv * jax.nn.sigmoid(conv) * u).astype(BF16)
        part = _dot(act, wdn_ref[cs, :])
        if fc == 0:
            acc_ref[...] = part
        else:
            acc_ref[...] += part
    y_ref[...] = x1 + acc_ref[...]


def _outproj_ffn(x, o, w_out, gain, w_up, conv_w, conv_b, w_down, *, tile, f_chunk):
    bsz, seq, d_model = x.shape
    ffn_dim = w_down.shape[0]
    tok = pl.BlockSpec((None, tile, d_model), lambda b, t: (b, t, 0))
    full = lambda shape: pl.BlockSpec(shape, lambda b, t: (0,) * len(shape))
    return pl.pallas_call(
        functools.partial(_ffn_kernel, f_chunk=f_chunk),
        grid=(bsz, seq // tile),
        in_specs=[
            tok, tok,
            full((d_model, d_model)),
            full((1, d_model)),
            full((d_model, 2 * ffn_dim)),
            full((CONV_WIDTH, ffn_dim)),
            full((1, ffn_dim)),
            full((ffn_dim, d_model)),
        ],
        out_specs=tok,
        out_shape=jax.ShapeDtypeStruct((bsz, seq, d_model), x.dtype),
        scratch_shapes=[
            pltpu.VMEM((CARRY_ROWS, ffn_dim), F32),
            pltpu.VMEM((tile, d_model), F32),
        ],
        compiler_params=_params(2),
        name="outproj_convglu",
    )(x, o, w_out, gain.reshape(1, d_model), w_up, conv_w, conv_b.reshape(1, ffn_dim), w_down)


def _rope_tables(seq):
    inv = 1.0 / (ROPE_THETA ** (jnp.arange(0, HEAD_DIM, 2, dtype=F32) / HEAD_DIM))
    ang = jnp.arange(seq, dtype=F32)[:, None] * inv[None, :]
    ang = jnp.concatenate([ang, ang], axis=-1)
    sign = jnp.where(jnp.arange(HEAD_DIM) < HEAD_DIM // 2, -1.0, 1.0).astype(F32)
    return jnp.cos(ang), jnp.sin(ang) * sign


def kernel(x, attn_norm, ffn_norm, hgrn_w_in, hgrn_lb, hgrn_out_norm, hgrn_w_out, moba_w_qkv, moba_q_norm, moba_k_norm, moba_w_out, ffn_w_up, ffn_conv_w, ffn_conv_b, ffn_w_down):
    bsz, seq, d_model = x.shape
    depth = attn_norm.shape[0]
    n_heads = d_model // HEAD_DIM
    n_blk = seq // MB_BLOCK
    assert d_model % HEAD_DIM == 0 and seq % MB_BLOCK == 0 and n_blk <= HEAD_DIM
    tile = 256
    cos, sin_signed = _rope_tables(seq)
    bf = lambda w: w.astype(BF16)
    hgrn_w_in, hgrn_w_out = bf(hgrn_w_in), bf(hgrn_w_out)
    moba_w_qkv, moba_w_out = bf(moba_w_qkv), bf(moba_w_out)
    ffn_w_up, ffn_w_down = bf(ffn_w_up), bf(ffn_w_down)

    for layer in range(depth):
        slot = layer // 2
        if layer % 2 == 0:
            o = _hgrn_mixer(x, attn_norm[layer], hgrn_w_in[slot], hgrn_lb, hgrn_out_norm[slot],
                            slot=slot, tile=tile)
            w_out = hgrn_w_out[slot]
        else:
            q, k, v, k_mean = _moba_qkv(x, attn_norm[layer], moba_w_qkv[slot], moba_q_norm[slot],
                                        moba_k_norm[slot], cos, sin_signed, tile=tile)
            k_mean = k_mean.reshape(bsz, n_blk, n_heads, HEAD_DIM).transpose(0, 2, 1, 3)
            k_mean = jnp.pad(k_mean, ((0, 0), (0, 0), (0, HEAD_DIM - n_blk), (0, 0)))
            o = _moba_attention(q, k, v, k_mean)
            w_out = moba_w_out[slot]
        x = _outproj_ffn(x, o, w_out, ffn_norm[layer], ffn_w_up[layer], ffn_conv_w[layer],
                         ffn_conv_b[layer], ffn_w_down[layer], tile=tile, f_chunk=512)
    return x
```

```python
import functools

import jax
import jax.numpy as jnp
from jax import lax
from jax.experimental import pallas as pl
from jax.experimental.pallas import tpu as pltpu

EPS = 1e-6
HEAD_DIM = 128
HG_CHUNK = 64
MB_BLOCK = 256
MB_TOPK = 3
ROPE_THETA = 10000.0
CONV_WIDTH = 3
CARRY_ROWS = 8

VMEM_LIMIT_BYTES = 56 * 1024 * 1024

F32 = jnp.float32
BF16 = jnp.bfloat16


def _dot(a, b):
    return jnp.dot(a, b, preferred_element_type=F32)


def _dot_nt(a, b):
    return lax.dot_general(a, b, (((1,), (1,)), ((), ())), preferred_element_type=F32)


def _dot_tn(a, b):
    return lax.dot_general(a, b, (((0,), (0,)), ((), ())), preferred_element_type=F32)


def _rms_norm(x, gain):
    return x * lax.rsqrt(jnp.mean(x * x, axis=-1, keepdims=True) + EPS) * gain


def _split3_bf16(x):
    hi = x.astype(BF16)
    r1 = x - hi.astype(F32)
    mid = r1.astype(BF16)
    lo = (r1 - mid.astype(F32)).astype(BF16)
    return hi, mid, lo


def _params(n_grid_axes):
    return pltpu.CompilerParams(
        dimension_semantics=("arbitrary",) * n_grid_axes,
        vmem_limit_bytes=VMEM_LIMIT_BYTES,
    )


def _hgrn_kernel(x_ref, gain_ref, win_ref, lb_ref, og_ref, o_ref, proj_ref, state_ref, *,
                 slot, n_heads):
    tile, d_model = x_ref.shape
    n_chunks = tile // HG_CHUNK

    @pl.when(pl.program_id(1) == 0)
    def _():
        state_ref[...] = jnp.zeros_like(state_ref)

    h = _rms_norm(x_ref[...], gain_ref[...]).astype(BF16)
    proj_ref[...] = _dot(h, win_ref[...])

    logits = lb_ref[...]
    rows = [logits[i:i + 1, :] for i in range(logits.shape[0])]
    top = functools.reduce(jnp.maximum, rows)
    exps = [jnp.exp(r - top) for r in rows]
    denom = functools.reduce(jnp.add, exps)
    cum = []
    for e in exps:
        p = e / denom
        cum.append(p if not cum else cum[-1] + p)
    lb = cum[slot] - cum[0]
    log_lb = jnp.log(lb)
    log_1m_lb = jnp.log1p(-lb)
    one_m_lb = 1.0 - lb

    r_i = lax.broadcasted_iota(jnp.int32, (HG_CHUNK, HG_CHUNK), 0)
    c_i = lax.broadcasted_iota(jnp.int32, (HG_CHUNK, HG_CHUNK), 1)
    causal = r_i >= c_i
    tril = causal.astype(BF16)
    og = og_ref[...]
    mid = HG_CHUNK // 2 - 1

    def chunk_body(c, carry):
        rows_c = pl.ds(pl.multiple_of(c * HG_CHUNK, HG_CHUNK), HG_CHUNK)
        q = proj_ref[rows_c, 0:d_model] * (HEAD_DIM ** -0.5)
        fz = proj_ref[rows_c, d_model:2 * d_model]
        v = proj_ref[rows_c, 2 * d_model:3 * d_model].astype(BF16)
        g = proj_ref[rows_c, 3 * d_model:4 * d_model]

        e_neg = jnp.exp(-jnp.abs(fz))
        log_sig = jnp.minimum(fz, 0.0) - jnp.log1p(e_neg)
        b = log_1m_lb + log_sig
        top_ab = jnp.maximum(log_lb, b)
        log_f = top_ab + jnp.log1p(jnp.exp(-jnp.abs(log_lb - b)))
        inv = 1.0 / (1.0 + e_neg)
        k = one_m_lb * jnp.where(fz >= 0.0, e_neg * inv, inv)

        hi, md, lo = _split3_bf16(log_f)
        G = _dot(tril, hi) + _dot(tril, md) + _dot(tril, lo)
        g_mid = G[mid:mid + 1, :]
        g_last = G[HG_CHUNK - 1:HG_CHUNK, :]
        q_rel = (q * jnp.exp(G - g_mid)).astype(BF16)
        k_rel = (k * jnp.exp(g_mid - G)).astype(BF16)
        q_in = (q * jnp.exp(G)).astype(BF16)
        k_st = (k * jnp.exp(g_last - G)).astype(BF16)
        decay = jnp.exp(g_last)
        gate = g * jax.nn.sigmoid(g)

        for hd in range(n_heads):
            sl = slice(hd * HEAD_DIM, (hd + 1) * HEAD_DIM)
            att = jnp.where(causal, _dot_nt(q_rel[:, sl], k_rel[:, sl]), 0.0).astype(BF16)
            o = _dot(att, v[:, sl])
            st = state_ref[hd]
            o = o + _dot_nt(q_in[:, sl], st.astype(BF16))
            state_ref[hd] = st * decay[:, sl] + _dot_tn(v[:, sl], k_st[:, sl])
            y = _rms_norm(o, og) * gate[:, sl]
            o_ref[rows_c, sl] = y.astype(o_ref.dtype)
        return carry

    lax.fori_loop(0, n_chunks, chunk_body, 0)


def _hgrn_mixer(x, gain, w_in, lb_logits, out_gain, *, slot, tile):
    bsz, seq, d_model = x.shape
    n_heads = d_model // HEAD_DIM
    n_layers = lb_logits.shape[0]
    kern = functools.partial(_hgrn_kernel, slot=slot, n_heads=n_heads)
    return pl.pallas_call(
        kern,
        grid=(bsz, seq // tile),
        in_specs=[
            pl.BlockSpec((None, tile, d_model), lambda b, t: (b, t, 0)),
            pl.BlockSpec((1, d_model), lambda b, t: (0, 0)),
            pl.BlockSpec((d_model, 4 * d_model), lambda b, t: (0, 0)),
            pl.BlockSpec((n_layers, d_model), lambda b, t: (0, 0)),
            pl.BlockSpec((1, HEAD_DIM), lambda b, t: (0, 0)),
        ],
        out_specs=pl.BlockSpec((None, tile, d_model), lambda b, t: (b, t, 0)),
        out_shape=jax.ShapeDtypeStruct((bsz, seq, d_model), BF16),
        scratch_shapes=[
            pltpu.VMEM((tile, 4 * d_model), F32),
            pltpu.VMEM((n_heads, HEAD_DIM, HEAD_DIM), F32),
        ],
        compiler_params=_params(2),
        name="hgrn_mixer",
    )(x, gain.reshape(1, d_model), w_in, lb_logits, out_gain.reshape(1, HEAD_DIM))


def _qkv_kernel(x_ref, gain_ref, w_ref, qg_ref, kg_ref, cos_ref, sin_ref,
                q_ref, k_ref, vt_ref, km_ref, *, n_heads):
    tile, d_model = x_ref.shape
    h = _rms_norm(x_ref[...], gain_ref[...]).astype(BF16)
    qkv = _dot(h, w_ref[...])
    cos = cos_ref[...]
    sin_signed = sin_ref[...]
    for j in range(tile // MB_BLOCK):
        vt_ref[j] = qkv[j * MB_BLOCK:(j + 1) * MB_BLOCK, 2 * d_model:3 * d_model].T.astype(vt_ref.dtype)
    for hd in range(n_heads):
        sl = slice(hd * HEAD_DIM, (hd + 1) * HEAD_DIM)
        qh = _rms_norm(qkv[:, hd * HEAD_DIM:(hd + 1) * HEAD_DIM], qg_ref[...])
        qh = (qh * cos + pltpu.roll(qh, HEAD_DIM // 2, axis=1) * sin_signed) * (HEAD_DIM ** -0.5)
        q_ref[:, sl] = qh.astype(q_ref.dtype)
        kh = _rms_norm(qkv[:, d_model + hd * HEAD_DIM:d_model + (hd + 1) * HEAD_DIM], kg_ref[...])
        kh = kh * cos + pltpu.roll(kh, HEAD_DIM // 2, axis=1) * sin_signed
        k_ref[:, sl] = kh.astype(k_ref.dtype)
        for j in range(tile // MB_BLOCK):
            km_ref[j, :, sl] = jnp.mean(kh[j * MB_BLOCK:(j + 1) * MB_BLOCK, :], axis=0, keepdims=True)


def _moba_qkv(x, gain, w_qkv, q_gain, k_gain, cos, sin_signed, *, tile):
    bsz, seq, d_model = x.shape
    n_heads = d_model // HEAD_DIM
    tok = pl.BlockSpec((None, tile, d_model), lambda b, t: (b, t, 0))
    rope = pl.BlockSpec((tile, HEAD_DIM), lambda b, t: (t, 0))
    vec = pl.BlockSpec((1, HEAD_DIM), lambda b, t: (0, 0))
    act = jax.ShapeDtypeStruct((bsz, seq, d_model), BF16)
    return pl.pallas_call(
        functools.partial(_qkv_kernel, n_heads=n_heads),
        grid=(bsz, seq // tile),
        in_specs=[
            tok,
            pl.BlockSpec((1, d_model), lambda b, t: (0, 0)),
            pl.BlockSpec((d_model, 3 * d_model), lambda b, t: (0, 0)),
            vec, vec, rope, rope,
        ],
        out_specs=[
            tok, tok,
            pl.BlockSpec((None, tile // MB_BLOCK, d_model, MB_BLOCK), lambda b, t: (b, t, 0, 0)),
            pl.BlockSpec((None, tile // MB_BLOCK, 1, d_model), lambda b, t: (b, t, 0, 0)),
        ],
        out_shape=[act, act,
                   jax.ShapeDtypeStruct((bsz, seq // MB_BLOCK, d_model, MB_BLOCK), BF16),
                   jax.ShapeDtypeStruct((bsz, seq // MB_BLOCK, 1, d_model), F32)],
        compiler_params=_params(2),
        name="moba_qkv",
    )(x, gain.reshape(1, d_model), w_qkv, q_gain.reshape(1, HEAD_DIM), k_gain.reshape(1, HEAD_DIM),
      cos, sin_signed)


def _moba_attn_kernel(q_ref, k_ref, vt_ref, km_ref, o_ref, sel_ref, *, heads_per_step):
    qb = pl.program_id(2)
    n_blk = km_ref.shape[1]
    neg_inf = -jnp.inf
    heads = range(heads_per_step)
    hs = [slice(hd * HEAD_DIM, (hd + 1) * HEAD_DIM) for hd in heads]
    qs = [q_ref[:, sl] for sl in hs]

    blk_i = lax.broadcasted_iota(jnp.int32, (n_blk, MB_BLOCK), 0)
    for hd in heads:
        km = km_ref[hd]
        km_hi = km.astype(BF16)
        km_lo = (km - km_hi.astype(F32)).astype(BF16)
        gate = _dot_nt(km_hi, qs[hd]) + _dot_nt(km_lo, qs[hd])
        gate = jnp.where(blk_i < qb, gate, neg_inf)
        sel = jnp.zeros(gate.shape, F32)
        for _ in range(MB_TOPK):
            top = jnp.max(gate, axis=0, keepdims=True)
            idx = jnp.min(jnp.where(gate == top, blk_i, n_blk), axis=0, keepdims=True)
            pick = (blk_i == idx) & (top > neg_inf)
            sel = jnp.where(pick, 1.0, sel)
            gate = jnp.where(pick, neg_inf, gate)
        sel_ref[hd] = sel

    own = pl.ds(pl.multiple_of(qb * MB_BLOCK, MB_BLOCK), MB_BLOCK)
    key_i = lax.broadcasted_iota(jnp.int32, (MB_BLOCK, MB_BLOCK), 0)
    qry_i = lax.broadcasted_iota(jnp.int32, (MB_BLOCK, MB_BLOCK), 1)
    init = []
    own_scores = [_dot_nt(k_ref[own, hs[hd]], qs[hd]) for hd in heads]
    for hd in heads:
        s = jnp.where(key_i <= qry_i, own_scores[hd], neg_inf)
        m = jnp.max(s, axis=0, keepdims=True)
        p = jnp.exp(s - m)
        init += [m, jnp.sum(p, axis=0, keepdims=True), _dot(vt_ref[qb, hs[hd], :], p.astype(BF16))]

    def body(j, carry):
        blk = pl.ds(pl.multiple_of(j * MB_BLOCK, MB_BLOCK), MB_BLOCK)
        out = []
        scores = [_dot_nt(k_ref[blk, hs[hd]], qs[hd]) for hd in heads]
        for hd in heads:
            m, l, acc = carry[3 * hd:3 * hd + 3]
            chosen = sel_ref[hd, pl.ds(j, 1), :] > 0.5
            s = jnp.where(chosen, scores[hd], neg_inf)
            m_new = jnp.maximum(m, jnp.max(s, axis=0, keepdims=True))
            alpha = jnp.exp(m - m_new)
            p = jnp.exp(s - m_new)
            l = alpha * l + jnp.sum(p, axis=0, keepdims=True)
            acc = alpha * acc + _dot(vt_ref[j, hs[hd], :], p.astype(BF16))
            out += [m_new, l, acc]
        return tuple(out)

    fin = lax.fori_loop(0, qb, body, tuple(init))
    for hd in heads:
        _, l, acc = fin[3 * hd:3 * hd + 3]
        o_ref[:, hs[hd]] = (acc / l).T.astype(o_ref.dtype)


def _moba_attention(q, k, v_t, k_mean, *, heads_per_step):
    bsz, seq, d_model = q.shape
    n_heads = d_model // HEAD_DIM
    n_blk = seq // MB_BLOCK
    width = heads_per_step * HEAD_DIM
    q_spec = pl.BlockSpec((None, MB_BLOCK, width), lambda b, h, i: (b, i, h))
    return pl.pallas_call(
        functools.partial(_moba_attn_kernel, heads_per_step=heads_per_step),
        grid=(bsz, n_heads // heads_per_step, n_blk),
        in_specs=[
            q_spec,
            pl.BlockSpec((None, seq, width), lambda b, h, i: (b, 0, h)),
            pl.BlockSpec((None, n_blk, width, MB_BLOCK), lambda b, h, i: (b, 0, h, 0)),
            pl.BlockSpec((None, heads_per_step, n_blk, HEAD_DIM), lambda b, h, i: (b, h, 0, 0)),
        ],
        out_specs=q_spec,
        out_shape=jax.ShapeDtypeStruct((bsz, seq, d_model), BF16),
        scratch_shapes=[pltpu.VMEM((heads_per_step, n_blk, MB_BLOCK), F32)],
        compiler_params=_params(3),
        name="moba_attention",
    )(q, k, v_t, k_mean)


def _ffn_kernel(x_ref, o_ref, wout_ref, gain_ref, wup_ref, cw_ref, cb_ref, wdn_ref,
                y_ref, carry_ref, acc_ref, *, f_chunk):
    tile, d_model = x_ref.shape
    ffn_dim = wdn_ref.shape[0]

    @pl.when(pl.program_id(1) == 0)
    def _():
        carry_ref[...] = jnp.zeros_like(carry_ref)

    x1 = x_ref[...] + _dot(o_ref[...], wout_ref[...])
    h = _rms_norm(x1, gain_ref[...]).astype(BF16)
    row = lax.broadcasted_iota(jnp.int32, (tile, f_chunk), 0)
    for fc in range(ffn_dim // f_chunk):
        cs = slice(fc * f_chunk, (fc + 1) * f_chunk)
        a = _dot(h, wup_ref[:, cs])
        u = _dot(h, wup_ref[:, ffn_dim + fc * f_chunk:ffn_dim + (fc + 1) * f_chunk])
        prev = carry_ref[:, cs]
        p1 = prev[CARRY_ROWS - 1:CARRY_ROWS, :]
        p2 = prev[CARRY_ROWS - 2:CARRY_ROWS - 1, :]
        a1 = jnp.where(row == 0, p1, pltpu.roll(a, 1, axis=0))
        a2 = jnp.where(row == 0, p2, jnp.where(row == 1, p1, pltpu.roll(a, 2, axis=0)))
        carry_ref[:, cs] = a[tile - CARRY_ROWS:tile, :]
        conv = cw_ref[0:1, cs] * a2 + cw_ref[1:2, cs] * a1 + cw_ref[2:3, cs] * a + cb_ref[:, cs]
        act = (conv * jax.nn.sigmoid(conv) * u).astype(BF16)
        part = _dot(act, wdn_ref[cs, :])
        if fc == 0:
            acc_ref[...] = part
        else:
            acc_ref[...] += part
    y_ref[...] = x1 + acc_ref[...]


def _outproj_ffn(x, o, w_out, gain, w_up, conv_w, conv_b, w_down, *, tile, f_chunk):
    bsz, seq, d_model = x.shape
    ffn_dim = w_down.shape[0]
    tok = pl.BlockSpec((None, tile, d_model), lambda b, t: (b, t, 0))
    full = lambda shape: pl.BlockSpec(shape, lambda b, t: (0,) * len(shape))
    return pl.pallas_call(
        functools.partial(_ffn_kernel, f_chunk=f_chunk),
        grid=(bsz, seq // tile),
        in_specs=[
            tok, tok,
            full((d_model, d_model)),
            full((1, d_model)),
            full((d_model, 2 * ffn_dim)),
            full((CONV_WIDTH, ffn_dim)),
            full((1, ffn_dim)),
            full((ffn_dim, d_model)),
        ],
        out_specs=tok,
        out_shape=jax.ShapeDtypeStruct((bsz, seq, d_model), x.dtype),
        scratch_shapes=[
            pltpu.VMEM((CARRY_ROWS, ffn_dim), F32),
            pltpu.VMEM((tile, d_model), F32),
        ],
        compiler_params=_params(2),
        name="outproj_convglu",
    )(x, o, w_out, gain.reshape(1, d_model), w_up, conv_w, conv_b.reshape(1, ffn_dim), w_down)


def _rope_tables(seq):
    inv = 1.0 / (ROPE_THETA ** (jnp.arange(0, HEAD_DIM, 2, dtype=F32) / HEAD_DIM))
    ang = jnp.arange(seq, dtype=F32)[:, None] * inv[None, :]
    ang = jnp.concatenate([ang, ang], axis=-1)
    sign = jnp.where(jnp.arange(HEAD_DIM) < HEAD_DIM // 2, -1.0, 1.0).astype(F32)
    return jnp.cos(ang), jnp.sin(ang) * sign


def kernel(x, attn_norm, ffn_norm, hgrn_w_in, hgrn_lb, hgrn_out_norm, hgrn_w_out, moba_w_qkv, moba_q_norm, moba_k_norm, moba_w_out, ffn_w_up, ffn_conv_w, ffn_conv_b, ffn_w_down):
    bsz, seq, d_model = x.shape
    depth = attn_norm.shape[0]
    n_heads = d_model // HEAD_DIM
    n_blk = seq // MB_BLOCK
    assert d_model % HEAD_DIM == 0 and seq % MB_BLOCK == 0 and n_blk % 8 == 0
    tile = 256
    cos, sin_signed = _rope_tables(seq)
    bf = lambda w: w.astype(BF16)
    hgrn_w_in, hgrn_w_out = bf(hgrn_w_in), bf(hgrn_w_out)
    moba_w_qkv, moba_w_out = bf(moba_w_qkv), bf(moba_w_out)
    ffn_w_up, ffn_w_down = bf(ffn_w_up), bf(ffn_w_down)

    for layer in range(depth):
        slot = layer // 2
        if layer % 2 == 0:
            o = _hgrn_mixer(x, attn_norm[layer], hgrn_w_in[slot], hgrn_lb, hgrn_out_norm[slot],
                            slot=slot, tile=tile)
            w_out = hgrn_w_out[slot]
        else:
            q, k, v_t, k_mean = _moba_qkv(x, attn_norm[layer], moba_w_qkv[slot], moba_q_norm[slot],
                                          moba_k_norm[slot], cos, sin_signed, tile=tile)
            k_mean = k_mean.reshape(bsz, n_blk, n_heads, HEAD_DIM).transpose(0, 2, 1, 3)
            o = _moba_attention(q, k, v_t, k_mean, heads_per_step=4)
            w_out = moba_w_out[slot]
        x = _outproj_ffn(x, o, w_out, ffn_norm[layer], ffn_w_up[layer], ffn_conv_w[layer],
                         ffn_conv_b[layer], ffn_w_down[layer], tile=tile, f_chunk=512)
    return x
```

```python
import functools

import jax
import jax.numpy as jnp
from jax import lax
from jax.experimental import pallas as pl
from jax.experimental.pallas import tpu as pltpu

EPS = 1e-6
HEAD_DIM = 128
HG_CHUNK = 64
MB_BLOCK = 256
MB_TOPK = 3
ROPE_THETA = 10000.0
CONV_WIDTH = 3
CARRY_ROWS = 8

VMEM_LIMIT_BYTES = 56 * 1024 * 1024

F32 = jnp.float32
BF16 = jnp.bfloat16


def _dot(a, b):
    return jnp.dot(a, b, preferred_element_type=F32)


def _dot_nt(a, b):
    return lax.dot_general(a, b, (((1,), (1,)), ((), ())), preferred_element_type=F32)


def _dot_tn(a, b):
    return lax.dot_general(a, b, (((0,), (0,)), ((), ())), preferred_element_type=F32)


def _rms_norm(x, gain):
    return x * lax.rsqrt(jnp.mean(x * x, axis=-1, keepdims=True) + EPS) * gain


def _params(n_grid_axes):
    return pltpu.CompilerParams(
        dimension_semantics=("arbitrary",) * n_grid_axes,
        vmem_limit_bytes=VMEM_LIMIT_BYTES,
    )


def _resident(shape):
    return pl.BlockSpec(shape, lambda *_: (0,) * len(shape), pipeline_mode=pl.Buffered(1))


def _hgrn_kernel(x_ref, gain_ref, win_ref, lb_ref, og_ref, o_ref, proj_ref, state_ref, *,
                 slot, n_heads):
    tile, d_model = x_ref.shape
    n_chunks = tile // HG_CHUNK

    @pl.when(pl.program_id(1) == 0)
    def _():
        state_ref[...] = jnp.zeros_like(state_ref)

    h = _rms_norm(x_ref[...], gain_ref[...]).astype(BF16)
    proj_ref[...] = _dot(h, win_ref[...])

    logits = lb_ref[...]
    rows = [logits[i:i + 1, :] for i in range(logits.shape[0])]
    top = functools.reduce(jnp.maximum, rows)
    exps = [jnp.exp(r - top) for r in rows]
    denom = functools.reduce(jnp.add, exps)
    cum = []
    for e in exps:
        p = e / denom
        cum.append(p if not cum else cum[-1] + p)
    lb = cum[slot] - cum[0]
    log_lb = jnp.log(lb)
    log_1m_lb = jnp.log1p(-lb)
    one_m_lb = 1.0 - lb

    r_i = lax.broadcasted_iota(jnp.int32, (HG_CHUNK, HG_CHUNK), 0)
    c_i = lax.broadcasted_iota(jnp.int32, (HG_CHUNK, HG_CHUNK), 1)
    causal = r_i >= c_i
    tril = causal.astype(BF16)
    og = og_ref[...]
    mid = HG_CHUNK // 2 - 1

    def chunk_body(c, carry):
        rows_c = pl.ds(pl.multiple_of(c * HG_CHUNK, HG_CHUNK), HG_CHUNK)
        q = proj_ref[rows_c, 0:d_model] * (HEAD_DIM ** -0.5)
        fz = proj_ref[rows_c, d_model:2 * d_model]
        v = proj_ref[rows_c, 2 * d_model:3 * d_model].astype(BF16)
        g = proj_ref[rows_c, 3 * d_model:4 * d_model]

        e_neg = jnp.exp(-jnp.abs(fz))
        log_sig = jnp.minimum(fz, 0.0) - jnp.log(1.0 + e_neg)
        b = log_1m_lb + log_sig
        top_ab = jnp.maximum(log_lb, b)
        log_f = top_ab + jnp.log(1.0 + jnp.exp(-jnp.abs(log_lb - b)))
        inv = 1.0 / (1.0 + e_neg)
        k = one_m_lb * jnp.where(fz >= 0.0, e_neg * inv, inv)

        hi = log_f.astype(BF16)
        lo = (log_f - hi.astype(F32)).astype(BF16)
        G = _dot(tril, hi) + _dot(tril, lo)
        g_mid = G[mid:mid + 1, :]
        g_last = G[HG_CHUNK - 1:HG_CHUNK, :]
        q_rel = (q * jnp.exp(G - g_mid)).astype(BF16)
        k_rel = (k * jnp.exp(g_mid - G)).astype(BF16)
        q_in = (q * jnp.exp(G)).astype(BF16)
        k_st = (k * jnp.exp(g_last - G)).astype(BF16)
        decay = jnp.exp(g_last)
        gate = g * jax.nn.sigmoid(g)

        hs = [slice(hd * HEAD_DIM, (hd + 1) * HEAD_DIM) for hd in range(n_heads)]
        att_raw = [_dot_nt(q_rel[:, sl], k_rel[:, sl]) for sl in hs]
        o_inter = []
        for hd, sl in enumerate(hs):
            st = state_ref[hd]
            o_inter.append(_dot_nt(q_in[:, sl], st.astype(BF16)))
            state_ref[hd] = st * decay[:, sl] + _dot_tn(v[:, sl], k_st[:, sl])
        for hd, sl in enumerate(hs):
            att = jnp.where(causal, att_raw[hd], 0.0).astype(BF16)
            o = _dot(att, v[:, sl]) + o_inter[hd]
            y = _rms_norm(o, og) * gate[:, sl]
            o_ref[rows_c, sl] = y.astype(o_ref.dtype)
        return carry

    lax.fori_loop(0, n_chunks, chunk_body, 0, unroll=2)


def _hgrn_mixer(x, gain, w_in, lb_logits, out_gain, *, slot, tile):
    bsz, seq, d_model = x.shape
    n_heads = d_model // HEAD_DIM
    n_layers = lb_logits.shape[0]
    kern = functools.partial(_hgrn_kernel, slot=slot, n_heads=n_heads)
    return pl.pallas_call(
        kern,
        grid=(bsz, seq // tile),
        in_specs=[
            pl.BlockSpec((None, tile, d_model), lambda b, t: (b, t, 0)),
            _resident((1, d_model)),
            _resident((d_model, 4 * d_model)),
            _resident((n_layers, d_model)),
            _resident((1, HEAD_DIM)),
        ],
        out_specs=pl.BlockSpec((None, tile, d_model), lambda b, t: (b, t, 0)),
        out_shape=jax.ShapeDtypeStruct((bsz, seq, d_model), BF16),
        scratch_shapes=[
            pltpu.VMEM((tile, 4 * d_model), F32),
            pltpu.VMEM((n_heads, HEAD_DIM, HEAD_DIM), F32),
        ],
        compiler_params=_params(2),
        name="hgrn_mixer",
    )(x, gain.reshape(1, d_model), w_in, lb_logits, out_gain.reshape(1, HEAD_DIM))


def _qkv_kernel(x_ref, gain_ref, w_ref, qg_ref, kg_ref, cos_ref, sin_ref,
                q_ref, k_ref, vt_ref, km_ref, *, n_heads):
    tile, d_model = x_ref.shape
    h = _rms_norm(x_ref[...], gain_ref[...]).astype(BF16)
    qkv = _dot(h, w_ref[...])
    cos = cos_ref[...]
    sin_signed = sin_ref[...]
    for j in range(tile // MB_BLOCK):
        vt_ref[j] = qkv[j * MB_BLOCK:(j + 1) * MB_BLOCK, 2 * d_model:3 * d_model].T.astype(vt_ref.dtype)
    for hd in range(n_heads):
        sl = slice(hd * HEAD_DIM, (hd + 1) * HEAD_DIM)
        qh = _rms_norm(qkv[:, hd * HEAD_DIM:(hd + 1) * HEAD_DIM], qg_ref[...])
        qh = (qh * cos + pltpu.roll(qh, HEAD_DIM // 2, axis=1) * sin_signed) * (HEAD_DIM ** -0.5)
        q_ref[:, sl] = qh.astype(q_ref.dtype)
        kh = _rms_norm(qkv[:, d_model + hd * HEAD_DIM:d_model + (hd + 1) * HEAD_DIM], kg_ref[...])
        kh = kh * cos + pltpu.roll(kh, HEAD_DIM // 2, axis=1) * sin_signed
        k_ref[:, sl] = kh.astype(k_ref.dtype)
        for j in range(tile // MB_BLOCK):
            km_ref[j, :, sl] = jnp.mean(kh[j * MB_BLOCK:(j + 1) * MB_BLOCK, :], axis=0, keepdims=True)


def _moba_qkv(x, gain, w_qkv, q_gain, k_gain, cos, sin_signed, *, tile):
    bsz, seq, d_model = x.shape
    n_heads = d_model // HEAD_DIM
    tok = pl.BlockSpec((None, tile, d_model), lambda b, t: (b, t, 0))
    rope = pl.BlockSpec((tile, HEAD_DIM), lambda b, t: (t, 0))
    vec = _resident((1, HEAD_DIM))
    act = jax.ShapeDtypeStruct((bsz, seq, d_model), BF16)
    return pl.pallas_call(
        functools.partial(_qkv_kernel, n_heads=n_heads),
        grid=(bsz, seq // tile),
        in_specs=[
            tok,
            _resident((1, d_model)),
            _resident((d_model, 3 * d_model)),
            vec, vec, rope, rope,
        ],
        out_specs=[
            tok, tok,
            pl.BlockSpec((None, tile // MB_BLOCK, d_model, MB_BLOCK), lambda b, t: (b, t, 0, 0)),
            pl.BlockSpec((None, tile // MB_BLOCK, 1, d_model), lambda b, t: (b, t, 0, 0)),
        ],
        out_shape=[act, act,
                   jax.ShapeDtypeStruct((bsz, seq // MB_BLOCK, d_model, MB_BLOCK), BF16),
                   jax.ShapeDtypeStruct((bsz, seq // MB_BLOCK, 1, d_model), F32)],
        compiler_params=_params(2),
        name="moba_qkv",
    )(x, gain.reshape(1, d_model), w_qkv, q_gain.reshape(1, HEAD_DIM), k_gain.reshape(1, HEAD_DIM),
      cos, sin_signed)


def _moba_attn_kernel(q_ref, k_ref, vt_ref, km_ref, o_ref, sel_ref, *, heads_per_step):
    qb = pl.program_id(2)
    n_blk = km_ref.shape[1]
    neg_inf = -jnp.inf
    heads = range(heads_per_step)
    hs = [slice(hd * HEAD_DIM, (hd + 1) * HEAD_DIM) for hd in heads]
    qs = [q_ref[:, sl] for sl in hs]

    blk_i = lax.broadcasted_iota(jnp.int32, (n_blk, MB_BLOCK), 0)
    for hd in heads:
        km = km_ref[hd]
        km_hi = km.astype(BF16)
        km_lo = (km - km_hi.astype(F32)).astype(BF16)
        gate = _dot_nt(km_hi, qs[hd]) + _dot_nt(km_lo, qs[hd])
        gate = jnp.where(blk_i < qb, gate, neg_inf)
        sel = jnp.zeros(gate.shape, F32)
        for _ in range(MB_TOPK):
            top = jnp.max(gate, axis=0, keepdims=True)
            idx = jnp.min(jnp.where(gate == top, blk_i, n_blk), axis=0, keepdims=True)
            pick = (blk_i == idx) & (top > neg_inf)
            sel = jnp.where(pick, 1.0, sel)
            gate = jnp.where(pick, neg_inf, gate)
        sel_ref[hd] = sel

    own = pl.ds(pl.multiple_of(qb * MB_BLOCK, MB_BLOCK), MB_BLOCK)
    key_i = lax.broadcasted_iota(jnp.int32, (MB_BLOCK, MB_BLOCK), 0)
    qry_i = lax.broadcasted_iota(jnp.int32, (MB_BLOCK, MB_BLOCK), 1)
    init = []
    own_scores = [_dot_nt(k_ref[own, hs[hd]], qs[hd]) for hd in heads]
    for hd in heads:
        s = jnp.where(key_i <= qry_i, own_scores[hd], neg_inf)
        m = jnp.max(s, axis=0, keepdims=True)
        p = jnp.exp(s - m)
        init += [m, jnp.sum(p, axis=0, keepdims=True), _dot(vt_ref[qb, hs[hd], :], p.astype(BF16))]

    def body(j, carry):
        blk = pl.ds(pl.multiple_of(j * MB_BLOCK, MB_BLOCK), MB_BLOCK)
        out = []
        scores = [_dot_nt(k_ref[blk, hs[hd]], qs[hd]) for hd in heads]
        for hd in heads:
            m, l, acc = carry[3 * hd:3 * hd + 3]
            chosen = sel_ref[hd, pl.ds(j, 1), :] > 0.5
            s = jnp.where(chosen, scores[hd], neg_inf)
            m_new = jnp.maximum(m, jnp.max(s, axis=0, keepdims=True))
            alpha = jnp.exp(m - m_new)
            p = jnp.exp(s - m_new)
            l = alpha * l + jnp.sum(p, axis=0, keepdims=True)
            acc = alpha * acc + _dot(vt_ref[j, hs[hd], :], p.astype(BF16))
            out += [m_new, l, acc]
        return tuple(out)

    fin = lax.fori_loop(0, qb, body, tuple(init))
    for hd in heads:
        _, l, acc = fin[3 * hd:3 * hd + 3]
        o_ref[:, hs[hd]] = (acc / l).T.astype(o_ref.dtype)


def _moba_attention(q, k, v_t, k_mean, *, heads_per_step):
    bsz, seq, d_model = q.shape
    n_heads = d_model // HEAD_DIM
    n_blk = seq // MB_BLOCK
    width = heads_per_step * HEAD_DIM
    q_spec = pl.BlockSpec((None, MB_BLOCK, width), lambda b, h, i: (b, i, h))
    return pl.pallas_call(
        functools.partial(_moba_attn_kernel, heads_per_step=heads_per_step),
        grid=(bsz, n_heads // heads_per_step, n_blk),
        in_specs=[
            q_spec,
            pl.BlockSpec((None, seq, width), lambda b, h, i: (b, 0, h)),
            pl.BlockSpec((None, n_blk, width, MB_BLOCK), lambda b, h, i: (b, 0, h, 0)),
            pl.BlockSpec((None, heads_per_step, n_blk, HEAD_DIM), lambda b, h, i: (b, h, 0, 0)),
        ],
        out_specs=q_spec,
        out_shape=jax.ShapeDtypeStruct((bsz, seq, d_model), BF16),
        scratch_shapes=[pltpu.VMEM((heads_per_step, n_blk, MB_BLOCK), F32)],
        compiler_params=_params(3),
        name="moba_attention",
    )(q, k, v_t, k_mean)


def _ffn_kernel(x_ref, o_ref, wout_ref, gain_ref, wup_ref, cw_ref, cb_ref, wdn_ref,
                y_ref, carry_ref, acc_ref, *, f_chunk):
    tile, d_model = x_ref.shape
    ffn_dim = wdn_ref.shape[0]

    @pl.when(pl.program_id(1) == 0)
    def _():
        carry_ref[...] = jnp.zeros_like(carry_ref)

    x1 = x_ref[...] + _dot(o_ref[...], wout_ref[...])
    h = _rms_norm(x1, gain_ref[...]).astype(BF16)
    row = lax.broadcasted_iota(jnp.int32, (tile, f_chunk), 0)
    n_chunks = ffn_dim // f_chunk

    def up_proj(fc):
        lo = fc * f_chunk
        return _dot(h, wup_ref[:, lo:lo + f_chunk]), _dot(h, wup_ref[:, ffn_dim + lo:ffn_dim + lo + f_chunk])

    ahead = up_proj(0)
    for fc in range(n_chunks):
        cs = slice(fc * f_chunk, (fc + 1) * f_chunk)
        a, u = ahead
        if fc + 1 < n_chunks:
            ahead = up_proj(fc + 1)
        prev = carry_ref[:, cs]
        p1 = prev[CARRY_ROWS - 1:CARRY_ROWS, :]
        p2 = prev[CARRY_ROWS - 2:CARRY_ROWS - 1, :]
        a1 = jnp.where(row == 0, p1, pltpu.roll(a, 1, axis=0))
        a2 = jnp.where(row == 0, p2, jnp.where(row == 1, p1, pltpu.roll(a, 2, axis=0)))
        carry_ref[:, cs] = a[tile - CARRY_ROWS:tile, :]
        conv = cw_ref[0:1, cs] * a2 + cw_ref[1:2, cs] * a1 + cw_ref[2:3, cs] * a + cb_ref[:, cs]
        act = (conv * jax.nn.sigmoid(conv) * u).astype(BF16)
        part = _dot(act, wdn_ref[cs, :])
        if fc == 0:
            acc_ref[...] = part
        else:
            acc_ref[...] += part
    y_ref[...] = x1 + acc_ref[...]


def _outproj_ffn(x, o, w_out, gain, w_up, conv_w, conv_b, w_down, *, tile, f_chunk):
    bsz, seq, d_model = x.shape
    ffn_dim = w_down.shape[0]
    tok = pl.BlockSpec((None, tile, d_model), lambda b, t: (b, t, 0))
    full = _resident
    return pl.pallas_call(
        functools.partial(_ffn_kernel, f_chunk=f_chunk),
        grid=(bsz, seq // tile),
        in_specs=[
            tok, tok,
            full((d_model, d_model)),
            full((1, d_model)),
            full((d_model, 2 * ffn_dim)),
            full((CONV_WIDTH, ffn_dim)),
            full((1, ffn_dim)),
            full((ffn_dim, d_model)),
        ],
        out_specs=tok,
        out_shape=jax.ShapeDtypeStruct((bsz, seq, d_model), x.dtype),
        scratch_shapes=[
            pltpu.VMEM((CARRY_ROWS, ffn_dim), F32),
            pltpu.VMEM((tile, d_model), F32),
        ],
        compiler_params=_params(2),
        name="outproj_convglu",
    )(x, o, w_out, gain.reshape(1, d_model), w_up, conv_w, conv_b.reshape(1, ffn_dim), w_down)


def _rope_tables(seq):
    inv = 1.0 / (ROPE_THETA ** (jnp.arange(0, HEAD_DIM, 2, dtype=F32) / HEAD_DIM))
    ang = jnp.arange(seq, dtype=F32)[:, None] * inv[None, :]
    ang = jnp.concatenate([ang, ang], axis=-1)
    sign = jnp.where(jnp.arange(HEAD_DIM) < HEAD_DIM // 2, -1.0, 1.0).astype(F32)
    return jnp.cos(ang), jnp.sin(ang) * sign


def kernel(x, attn_norm, ffn_norm, hgrn_w_in, hgrn_lb, hgrn_out_norm, hgrn_w_out, moba_w_qkv, moba_q_norm, moba_k_norm, moba_w_out, ffn_w_up, ffn_conv_w, ffn_conv_b, ffn_w_down):
    bsz, seq, d_model = x.shape
    depth = attn_norm.shape[0]
    n_heads = d_model // HEAD_DIM
    n_blk = seq // MB_BLOCK
    assert d_model % HEAD_DIM == 0 and seq % MB_BLOCK == 0 and n_blk % 8 == 0
    tile = min(512, seq)
    cos, sin_signed = _rope_tables(seq)
    bf = lambda w: w.astype(BF16)
    hgrn_w_in, hgrn_w_out = bf(hgrn_w_in), bf(hgrn_w_out)
    moba_w_qkv, moba_w_out = bf(moba_w_qkv), bf(moba_w_out)
    ffn_w_up, ffn_w_down = bf(ffn_w_up), bf(ffn_w_down)

    for layer in range(depth):
        slot = layer // 2
        if layer % 2 == 0:
            o = _hgrn_mixer(x, attn_norm[layer], hgrn_w_in[slot], hgrn_lb, hgrn_out_norm[slot],
                            slot=slot, tile=tile)
            w_out = hgrn_w_out[slot]
        else:
            q, k, v_t, k_mean = _moba_qkv(x, attn_norm[layer], moba_w_qkv[slot], moba_q_norm[slot],
                                          moba_k_norm[slot], cos, sin_signed, tile=MB_BLOCK)
            k_mean = k_mean.reshape(bsz, n_blk, n_heads, HEAD_DIM).transpose(0, 2, 1, 3)
            o = _moba_attention(q, k, v_t, k_mean, heads_per_step=n_heads)
            w_out = moba_w_out[slot]
        x = _outproj_ffn(x, o, w_out, ffn_norm[layer], ffn_w_up[layer], ffn_conv_w[layer],
                         ffn_conv_b[layer], ffn_w_down[layer], tile=tile, f_chunk=512)
    return x
```

```python
import functools

import jax
import jax.numpy as jnp
from jax import lax
from jax.experimental import pallas as pl
from jax.experimental.pallas import tpu as pltpu

EPS = 1e-6
HEAD_DIM = 128
HG_CHUNK = 64
MB_BLOCK = 256
MB_TOPK = 3
ROPE_THETA = 10000.0
CONV_WIDTH = 3
CARRY_ROWS = 8

VMEM_LIMIT_BYTES = 56 * 1024 * 1024

F32 = jnp.float32
BF16 = jnp.bfloat16


def _dot(a, b):
    return jnp.dot(a, b, preferred_element_type=F32)


def _dot_nt(a, b):
    return lax.dot_general(a, b, (((1,), (1,)), ((), ())), preferred_element_type=F32)


def _dot_tn(a, b):
    return lax.dot_general(a, b, (((0,), (0,)), ((), ())), preferred_element_type=F32)


def _rms_norm(x, gain):
    return x * lax.rsqrt(jnp.mean(x * x, axis=-1, keepdims=True) + EPS) * gain


def _params(n_grid_axes):
    return pltpu.CompilerParams(
        dimension_semantics=("arbitrary",) * n_grid_axes,
        vmem_limit_bytes=VMEM_LIMIT_BYTES,
    )


def _resident(shape):
    return pl.BlockSpec(shape, lambda *_: (0,) * len(shape), pipeline_mode=pl.Buffered(1))


def _hgrn_kernel(x_ref, gain_ref, win_ref, lb_ref, og_ref, o_ref, proj_ref, state_ref, *,
                 slot, n_heads):
    tile, d_model = x_ref.shape
    n_chunks = tile // HG_CHUNK

    @pl.when(pl.program_id(1) == 0)
    def _():
        state_ref[...] = jnp.zeros_like(state_ref)

    h = _rms_norm(x_ref[...], gain_ref[...]).astype(BF16)
    proj_ref[...] = _dot(h, win_ref[...])

    logits = lb_ref[...]
    rows = [logits[i:i + 1, :] for i in range(logits.shape[0])]
    top = functools.reduce(jnp.maximum, rows)
    exps = [jnp.exp(r - top) for r in rows]
    denom = functools.reduce(jnp.add, exps)
    cum = []
    for e in exps:
        p = e / denom
        cum.append(p if not cum else cum[-1] + p)
    lb = cum[slot] - cum[0]
    one_m_lb = 1.0 - lb

    r_i = lax.broadcasted_iota(jnp.int32, (HG_CHUNK, HG_CHUNK), 0)
    c_i = lax.broadcasted_iota(jnp.int32, (HG_CHUNK, HG_CHUNK), 1)
    causal = r_i >= c_i
    tril = causal.astype(BF16)
    og = og_ref[...]
    mid = HG_CHUNK // 2 - 1

    def chunk_body(c, carry):
        rows_c = pl.ds(pl.multiple_of(c * HG_CHUNK, HG_CHUNK), HG_CHUNK)
        q = proj_ref[rows_c, 0:d_model] * (HEAD_DIM ** -0.5)
        fz = proj_ref[rows_c, d_model:2 * d_model]
        v = proj_ref[rows_c, 2 * d_model:3 * d_model].astype(BF16)
        g = proj_ref[rows_c, 3 * d_model:4 * d_model]

        e_neg = jnp.exp(-jnp.abs(fz))
        inv = 1.0 / (1.0 + e_neg)
        small = e_neg * inv
        pos = fz >= 0.0
        log_f = jnp.log(lb + one_m_lb * jnp.where(pos, inv, small))
        k = one_m_lb * jnp.where(pos, small, inv)

        hi = log_f.astype(BF16)
        lo = (log_f - hi.astype(F32)).astype(BF16)
        G = _dot(tril, hi) + _dot(tril, lo)
        g_mid = G[mid:mid + 1, :]
        g_last = G[HG_CHUNK - 1:HG_CHUNK, :]
        q_rel = (q * jnp.exp(G - g_mid)).astype(BF16)
        k_rel = (k * jnp.exp(g_mid - G)).astype(BF16)
        q_in = (q * jnp.exp(G)).astype(BF16)
        k_st = (k * jnp.exp(g_last - G)).astype(BF16)
        decay = jnp.exp(g_last)
        gate = g * jax.nn.sigmoid(g)

        hs = [slice(hd * HEAD_DIM, (hd + 1) * HEAD_DIM) for hd in range(n_heads)]
        att_raw = [_dot_nt(q_rel[:, sl], k_rel[:, sl]) for sl in hs]
        o_inter = []
        for hd, sl in enumerate(hs):
            st = state_ref[hd]
            o_inter.append(_dot_nt(q_in[:, sl], st.astype(BF16)))
            state_ref[hd] = st * decay[:, sl] + _dot_tn(v[:, sl], k_st[:, sl])
        for hd, sl in enumerate(hs):
            att = jnp.where(causal, att_raw[hd], 0.0).astype(BF16)
            o = _dot(att, v[:, sl]) + o_inter[hd]
            y = _rms_norm(o, og) * gate[:, sl]
            o_ref[rows_c, sl] = y.astype(o_ref.dtype)
        return carry

    lax.fori_loop(0, n_chunks, chunk_body, 0, unroll=2)


def _hgrn_mixer(x, gain, w_in, lb_logits, out_gain, *, slot, tile):
    bsz, seq, d_model = x.shape
    n_heads = d_model // HEAD_DIM
    n_layers = lb_logits.shape[0]
    kern = functools.partial(_hgrn_kernel, slot=slot, n_heads=n_heads)
    return pl.pallas_call(
        kern,
        grid=(bsz, seq // tile),
        in_specs=[
            pl.BlockSpec((None, tile, d_model), lambda b, t: (b, t, 0)),
            _resident((1, d_model)),
            _resident((d_model, 4 * d_model)),
            _resident((n_layers, d_model)),
            _resident((1, HEAD_DIM)),
        ],
        out_specs=pl.BlockSpec((None, tile, d_model), lambda b, t: (b, t, 0)),
        out_shape=jax.ShapeDtypeStruct((bsz, seq, d_model), BF16),
        scratch_shapes=[
            pltpu.VMEM((tile, 4 * d_model), F32),
            pltpu.VMEM((n_heads, HEAD_DIM, HEAD_DIM), F32),
        ],
        compiler_params=_params(2),
        name="hgrn_mixer",
    )(x, gain.reshape(1, d_model), w_in, lb_logits, out_gain.reshape(1, HEAD_DIM))


def _qkv_kernel(x_ref, gain_ref, w_ref, qg_ref, kg_ref, cos_ref, sin_ref,
                q_ref, k_ref, vt_ref, km_ref, *, n_heads):
    tile, d_model = x_ref.shape
    h = _rms_norm(x_ref[...], gain_ref[...]).astype(BF16)
    qkv = _dot(h, w_ref[...])
    cos = cos_ref[...]
    sin_signed = sin_ref[...]
    for j in range(tile // MB_BLOCK):
        vt_ref[j] = qkv[j * MB_BLOCK:(j + 1) * MB_BLOCK, 2 * d_model:3 * d_model].T.astype(vt_ref.dtype)
    for hd in range(n_heads):
        sl = slice(hd * HEAD_DIM, (hd + 1) * HEAD_DIM)
        qh = _rms_norm(qkv[:, hd * HEAD_DIM:(hd + 1) * HEAD_DIM], qg_ref[...])
        qh = (qh * cos + pltpu.roll(qh, HEAD_DIM // 2, axis=1) * sin_signed) * (HEAD_DIM ** -0.5)
        q_ref[:, sl] = qh.astype(q_ref.dtype)
        kh = _rms_norm(qkv[:, d_model + hd * HEAD_DIM:d_model + (hd + 1) * HEAD_DIM], kg_ref[...])
        kh = kh * cos + pltpu.roll(kh, HEAD_DIM // 2, axis=1) * sin_signed
        k_ref[:, sl] = kh.astype(k_ref.dtype)
        for j in range(tile // MB_BLOCK):
            km_ref[j, :, sl] = jnp.mean(kh[j * MB_BLOCK:(j + 1) * MB_BLOCK, :], axis=0, keepdims=True)


def _moba_qkv(x, gain, w_qkv, q_gain, k_gain, cos, sin_signed, *, tile):
    bsz, seq, d_model = x.shape
    n_heads = d_model // HEAD_DIM
    tok = pl.BlockSpec((None, tile, d_model), lambda b, t: (b, t, 0))
    rope = pl.BlockSpec((tile, HEAD_DIM), lambda b, t: (t, 0))
    vec = _resident((1, HEAD_DIM))
    act = jax.ShapeDtypeStruct((bsz, seq, d_model), BF16)
    return pl.pallas_call(
        functools.partial(_qkv_kernel, n_heads=n_heads),
        grid=(bsz, seq // tile),
        in_specs=[
            tok,
            _resident((1, d_model)),
            _resident((d_model, 3 * d_model)),
            vec, vec, rope, rope,
        ],
        out_specs=[
            tok, tok,
            pl.BlockSpec((None, tile // MB_BLOCK, d_model, MB_BLOCK), lambda b, t: (b, t, 0, 0)),
            pl.BlockSpec((None, tile // MB_BLOCK, 1, d_model), lambda b, t: (b, t, 0, 0)),
        ],
        out_shape=[act, act,
                   jax.ShapeDtypeStruct((bsz, seq // MB_BLOCK, d_model, MB_BLOCK), BF16),
                   jax.ShapeDtypeStruct((bsz, seq // MB_BLOCK, 1, d_model), F32)],
        compiler_params=_params(2),
        name="moba_qkv",
    )(x, gain.reshape(1, d_model), w_qkv, q_gain.reshape(1, HEAD_DIM), k_gain.reshape(1, HEAD_DIM),
      cos, sin_signed)


def _moba_attn_kernel(q_ref, k_ref, vt_ref, km_ref, o_ref, sel_ref, *, heads_per_step):
    qb = pl.program_id(2)
    n_blk = km_ref.shape[1]
    neg_inf = -jnp.inf
    heads = range(heads_per_step)
    hs = [slice(hd * HEAD_DIM, (hd + 1) * HEAD_DIM) for hd in heads]
    qs = [q_ref[:, sl] for sl in hs]

    blk_i = lax.broadcasted_iota(jnp.int32, (n_blk, MB_BLOCK), 0)
    for hd in heads:
        km = km_ref[hd]
        km_hi = km.astype(BF16)
        km_lo = (km - km_hi.astype(F32)).astype(BF16)
        gate = _dot_nt(km_hi, qs[hd]) + _dot_nt(km_lo, qs[hd])
        gate = jnp.where(blk_i < qb, gate, neg_inf)
        sel = jnp.zeros(gate.shape, F32)
        for _ in range(MB_TOPK):
            top = jnp.max(gate, axis=0, keepdims=True)
            idx = jnp.min(jnp.where(gate == top, blk_i, n_blk), axis=0, keepdims=True)
            pick = (blk_i == idx) & (top > neg_inf)
            sel = jnp.where(pick, 1.0, sel)
            gate = jnp.where(pick, neg_inf, gate)
        sel_ref[hd] = sel

    own = pl.ds(pl.multiple_of(qb * MB_BLOCK, MB_BLOCK), MB_BLOCK)
    key_i = lax.broadcasted_iota(jnp.int32, (MB_BLOCK, MB_BLOCK), 0)
    qry_i = lax.broadcasted_iota(jnp.int32, (MB_BLOCK, MB_BLOCK), 1)
    init = []
    own_scores = [_dot_nt(k_ref[own, hs[hd]], qs[hd]) for hd in heads]
    for hd in heads:
        s = jnp.where(key_i <= qry_i, own_scores[hd], neg_inf)
        m = jnp.max(s, axis=0, keepdims=True)
        p = jnp.exp(s - m)
        init += [m, jnp.sum(p, axis=0, keepdims=True), _dot(vt_ref[qb, hs[hd], :], p.astype(BF16))]

    def body(j, carry):
        blk = pl.ds(pl.multiple_of(j * MB_BLOCK, MB_BLOCK), MB_BLOCK)
        out = []
        scores = [_dot_nt(k_ref[blk, hs[hd]], qs[hd]) for hd in heads]
        for hd in heads:
            m, l, acc = carry[3 * hd:3 * hd + 3]
            chosen = sel_ref[hd, pl.ds(j, 1), :] > 0.5
            s = jnp.where(chosen, scores[hd], neg_inf)
            m_new = jnp.maximum(m, jnp.max(s, axis=0, keepdims=True))
            alpha = jnp.exp(m - m_new)
            p = jnp.exp(s - m_new)
            l = alpha * l + jnp.sum(p, axis=0, keepdims=True)
            acc = alpha * acc + _dot(vt_ref[j, hs[hd], :], p.astype(BF16))
            out += [m_new, l, acc]
        return tuple(out)

    fin = lax.fori_loop(0, qb, body, tuple(init))
    for hd in heads:
        _, l, acc = fin[3 * hd:3 * hd + 3]
        o_ref[:, hs[hd]] = (acc / l).T.astype(o_ref.dtype)


def _moba_attention(q, k, v_t, k_mean, *, heads_per_step):
    bsz, seq, d_model = q.shape
    n_heads = d_model // HEAD_DIM
    n_blk = seq // MB_BLOCK
    width = heads_per_step * HEAD_DIM
    q_spec = pl.BlockSpec((None, MB_BLOCK, width), lambda b, h, i: (b, i, h))
    return pl.pallas_call(
        functools.partial(_moba_attn_kernel, heads_per_step=heads_per_step),
        grid=(bsz, n_heads // heads_per_step, n_blk),
        in_specs=[
            q_spec,
            pl.BlockSpec((None, seq, width), lambda b, h, i: (b, 0, h)),
            pl.BlockSpec((None, n_blk, width, MB_BLOCK), lambda b, h, i: (b, 0, h, 0)),
            pl.BlockSpec((None, heads_per_step, n_blk, HEAD_DIM), lambda b, h, i: (b, h, 0, 0)),
        ],
        out_specs=q_spec,
        out_shape=jax.ShapeDtypeStruct((bsz, seq, d_model), BF16),
        scratch_shapes=[pltpu.VMEM((heads_per_step, n_blk, MB_BLOCK), F32)],
        compiler_params=_params(3),
        name="moba_attention",
    )(q, k, v_t, k_mean)


def _ffn_kernel(x_ref, o_ref, wout_ref, gain_ref, wup_ref, cw_ref, cb_ref, wdn_ref,
                y_ref, carry_ref, acc_ref, *, f_chunk):
    tile, d_model = x_ref.shape
    ffn_dim = wdn_ref.shape[0]

    @pl.when(pl.program_id(1) == 0)
    def _():
        carry_ref[...] = jnp.zeros_like(carry_ref)

    x1 = x_ref[...] + _dot(o_ref[...], wout_ref[...])
    h = _rms_norm(x1, gain_ref[...]).astype(BF16)
    row = lax.broadcasted_iota(jnp.int32, (tile, f_chunk), 0)
    n_chunks = ffn_dim // f_chunk

    def up_proj(fc):
        lo = fc * f_chunk
        return _dot(h, wup_ref[:, lo:lo + f_chunk]), _dot(h, wup_ref[:, ffn_dim + lo:ffn_dim + lo + f_chunk])

    ahead = up_proj(0)
    for fc in range(n_chunks):
        cs = slice(fc * f_chunk, (fc + 1) * f_chunk)
        a, u = ahead
        if fc + 1 < n_chunks:
            ahead = up_proj(fc + 1)
        prev = carry_ref[:, cs]
        p1 = prev[CARRY_ROWS - 1:CARRY_ROWS, :]
        p2 = prev[CARRY_ROWS - 2:CARRY_ROWS - 1, :]
        a1 = jnp.where(row == 0, p1, pltpu.roll(a, 1, axis=0))
        a2 = jnp.where(row == 0, p2, jnp.where(row == 1, p1, pltpu.roll(a, 2, axis=0)))
        carry_ref[:, cs] = a[tile - CARRY_ROWS:tile, :]
        conv = cw_ref[0:1, cs] * a2 + cw_ref[1:2, cs] * a1 + cw_ref[2:3, cs] * a + cb_ref[:, cs]
        act = (conv * jax.nn.sigmoid(conv) * u).astype(BF16)
        part = _dot(act, wdn_ref[cs, :])
        if fc == 0:
            acc_ref[...] = part
        else:
            acc_ref[...] += part
    y_ref[...] = x1 + acc_ref[...]


def _outproj_ffn(x, o, w_out, gain, w_up, conv_w, conv_b, w_down, *, tile, f_chunk):
    bsz, seq, d_model = x.shape
    ffn_dim = w_down.shape[0]
    tok = pl.BlockSpec((None, tile, d_model), lambda b, t: (b, t, 0))
    full = _resident
    return pl.pallas_call(
        functools.partial(_ffn_kernel, f_chunk=f_chunk),
        grid=(bsz, seq // tile),
        in_specs=[
            tok, tok,
            full((d_model, d_model)),
            full((1, d_model)),
            full((d_model, 2 * ffn_dim)),
            full((CONV_WIDTH, ffn_dim)),
            full((1, ffn_dim)),
            full((ffn_dim, d_model)),
        ],
        out_specs=tok,
        out_shape=jax.ShapeDtypeStruct((bsz, seq, d_model), x.dtype),
        scratch_shapes=[
            pltpu.VMEM((CARRY_ROWS, ffn_dim), F32),
            pltpu.VMEM((tile, d_model), F32),
        ],
        compiler_params=_params(2),
        name="outproj_convglu",
    )(x, o, w_out, gain.reshape(1, d_model), w_up, conv_w, conv_b.reshape(1, ffn_dim), w_down)


def _rope_tables(seq):
    inv = 1.0 / (ROPE_THETA ** (jnp.arange(0, HEAD_DIM, 2, dtype=F32) / HEAD_DIM))
    ang = jnp.arange(seq, dtype=F32)[:, None] * inv[None, :]
    ang = jnp.concatenate([ang, ang], axis=-1)
    sign = jnp.where(jnp.arange(HEAD_DIM) < HEAD_DIM // 2, -1.0, 1.0).astype(F32)
    return jnp.cos(ang), jnp.sin(ang) * sign


def kernel(x, attn_norm, ffn_norm, hgrn_w_in, hgrn_lb, hgrn_out_norm, hgrn_w_out, moba_w_qkv, moba_q_norm, moba_k_norm, moba_w_out, ffn_w_up, ffn_conv_w, ffn_conv_b, ffn_w_down):
    bsz, seq, d_model = x.shape
    depth = attn_norm.shape[0]
    n_heads = d_model // HEAD_DIM
    n_blk = seq // MB_BLOCK
    assert d_model % HEAD_DIM == 0 and seq % MB_BLOCK == 0 and n_blk % 8 == 0
    tile = min(512, seq)
    cos, sin_signed = _rope_tables(seq)
    bf = lambda w: w.astype(BF16)

    for layer in range(depth):
        slot = layer // 2
        if layer % 2 == 0:
            o = _hgrn_mixer(x, attn_norm[layer], bf(hgrn_w_in[slot]), hgrn_lb, hgrn_out_norm[slot],
                            slot=slot, tile=tile)
            w_out = bf(hgrn_w_out[slot])
        else:
            q, k, v_t, k_mean = _moba_qkv(x, attn_norm[layer], bf(moba_w_qkv[slot]), moba_q_norm[slot],
                                          moba_k_norm[slot], cos, sin_signed, tile=MB_BLOCK)
            k_mean = k_mean.reshape(bsz, n_blk, n_heads, HEAD_DIM).transpose(0, 2, 1, 3)
            o = _moba_attention(q, k, v_t, k_mean, heads_per_step=n_heads)
            w_out = bf(moba_w_out[slot])
        x = _outproj_ffn(x, o, w_out, ffn_norm[layer], bf(ffn_w_up[layer]), ffn_conv_w[layer],
                         ffn_conv_b[layer], bf(ffn_w_down[layer]), tile=tile, f_chunk=1024)
    return x
```

```python
import functools

import jax
import jax.numpy as jnp
from jax import lax
from jax.experimental import pallas as pl
from jax.experimental.pallas import tpu as pltpu

EPS = 1e-6
HEAD_DIM = 128
HG_CHUNK = 64
MB_BLOCK = 256
MB_TOPK = 3
ROPE_THETA = 10000.0
CONV_WIDTH = 3
CARRY_ROWS = 8
Q_SCALE = HEAD_DIM ** -0.5 * 1.4426950408889634
VMEM_LIMIT_BYTES = 56 * 1024 * 1024

F32 = jnp.float32
BF16 = jnp.bfloat16


def _dot(a, b):
    return jnp.dot(a, b, preferred_element_type=F32)


def _dot_nt(a, b):
    return lax.dot_general(a, b, (((1,), (1,)), ((), ())), preferred_element_type=F32)


def _dot_tn(a, b):
    return lax.dot_general(a, b, (((0,), (0,)), ((), ())), preferred_element_type=F32)


def _rms_norm(x, gain):
    return x * lax.rsqrt(jnp.mean(x * x, axis=-1, keepdims=True) + EPS) * gain


def _params(n_grid_axes):
    return pltpu.CompilerParams(
        dimension_semantics=("arbitrary",) * n_grid_axes,
        vmem_limit_bytes=VMEM_LIMIT_BYTES,
    )


def _resident(shape):
    return pl.BlockSpec(shape, lambda *_: (0,) * len(shape), pipeline_mode=pl.Buffered(1))


def _hgrn_kernel(x_ref, gain_ref, win_ref, lb_ref, og_ref, o_ref, proj_ref, state_ref, *,
                 slot, n_heads):
    tile, d_model = x_ref.shape
    n_chunks = tile // HG_CHUNK

    @pl.when(pl.program_id(1) == 0)
    def _():
        state_ref[...] = jnp.zeros_like(state_ref)

    h = _rms_norm(x_ref[...], gain_ref[...]).astype(BF16)
    proj_ref[...] = _dot(h, win_ref[...])

    logits = lb_ref[...]
    rows = [logits[i:i + 1, :] for i in range(logits.shape[0])]
    top = functools.reduce(jnp.maximum, rows)
    exps = [jnp.exp(r - top) for r in rows]
    denom = functools.reduce(jnp.add, exps)
    cum = []
    for e in exps:
        p = e / denom
        cum.append(p if not cum else cum[-1] + p)
    lb = cum[slot] - cum[0]
    one_m_lb = 1.0 - lb

    r_i = lax.broadcasted_iota(jnp.int32, (HG_CHUNK, HG_CHUNK), 0)
    c_i = lax.broadcasted_iota(jnp.int32, (HG_CHUNK, HG_CHUNK), 1)
    causal = r_i >= c_i
    tril = causal.astype(BF16)
    og = og_ref[...]
    mid = HG_CHUNK // 2 - 1

    def chunk_body(c, carry):
        rows_c = pl.ds(pl.multiple_of(c * HG_CHUNK, HG_CHUNK), HG_CHUNK)
        q = proj_ref[rows_c, 0:d_model] * (HEAD_DIM ** -0.5)
        fz = proj_ref[rows_c, d_model:2 * d_model]
        v = proj_ref[rows_c, 2 * d_model:3 * d_model].astype(BF16)
        g = proj_ref[rows_c, 3 * d_model:4 * d_model]

        e_neg = jnp.exp(-jnp.abs(fz))
        inv = 1.0 / (1.0 + e_neg)
        small = e_neg * inv
        pos = fz >= 0.0
        log_f = jnp.log(lb + one_m_lb * jnp.where(pos, inv, small))
        k = one_m_lb * jnp.where(pos, small, inv)

        hi = log_f.astype(BF16)
        lo = (log_f - hi.astype(F32)).astype(BF16)
        G = _dot(tril, hi) + _dot(tril, lo)
        g_mid = G[mid:mid + 1, :]
        g_last = G[HG_CHUNK - 1:HG_CHUNK, :]
        q_rel = (q * jnp.exp(G - g_mid)).astype(BF16)
        k_rel = (k * jnp.exp(g_mid - G)).astype(BF16)
        q_in = (q * jnp.exp(G)).astype(BF16)
        k_st = (k * jnp.exp(g_last - G)).astype(BF16)
        decay = jnp.exp(g_last)
        gate = g * jax.nn.sigmoid(g)

        hs = [slice(hd * HEAD_DIM, (hd + 1) * HEAD_DIM) for hd in range(n_heads)]
        att_raw = [_dot_nt(q_rel[:, sl], k_rel[:, sl]) for sl in hs]
        o_inter = []
        for hd, sl in enumerate(hs):
            st = state_ref[hd]
            o_inter.append(_dot_nt(q_in[:, sl], st.astype(BF16)))
            state_ref[hd] = st * decay[:, sl] + _dot_tn(v[:, sl], k_st[:, sl])
        for hd, sl in enumerate(hs):
            att = jnp.where(causal, att_raw[hd], 0.0).astype(BF16)
            o = _dot(att, v[:, sl]) + o_inter[hd]
            y = _rms_norm(o, og) * gate[:, sl]
            o_ref[rows_c, sl] = y.astype(o_ref.dtype)
        return carry

    lax.fori_loop(0, n_chunks, chunk_body, 0, unroll=2)


def _hgrn_mixer(x, gain, w_in, lb_logits, out_gain, *, slot, tile):
    bsz, seq, d_model = x.shape
    n_heads = d_model // HEAD_DIM
    n_layers = lb_logits.shape[0]
    kern = functools.partial(_hgrn_kernel, slot=slot, n_heads=n_heads)
    return pl.pallas_call(
        kern,
        grid=(bsz, seq // tile),
        in_specs=[
            pl.BlockSpec((None, tile, d_model), lambda b, t: (b, t, 0)),
            _resident((1, d_model)),
            _resident((d_model, 4 * d_model)),
            _resident((n_layers, d_model)),
            _resident((1, HEAD_DIM)),
        ],
        out_specs=pl.BlockSpec((None, tile, d_model), lambda b, t: (b, t, 0)),
        out_shape=jax.ShapeDtypeStruct((bsz, seq, d_model), BF16),
        scratch_shapes=[
            pltpu.VMEM((tile, 4 * d_model), F32),
            pltpu.VMEM((n_heads, HEAD_DIM, HEAD_DIM), F32),
        ],
        compiler_params=_params(2),
        name="hgrn_mixer",
    )(x, gain.reshape(1, d_model), w_in, lb_logits, out_gain.reshape(1, HEAD_DIM))


def _qkv_kernel(x_ref, gain_ref, w_ref, qg_ref, kg_ref, cos_ref, sin_ref,
                q_ref, k_ref, vt_ref, km_ref, *, n_heads):
    tile, d_model = x_ref.shape
    h = _rms_norm(x_ref[...], gain_ref[...]).astype(BF16)
    qkv = _dot(h, w_ref[...])
    cos = cos_ref[...]
    sin_signed = sin_ref[...]
    for j in range(tile // MB_BLOCK):
        vt_ref[j] = qkv[j * MB_BLOCK:(j + 1) * MB_BLOCK, 2 * d_model:3 * d_model].T.astype(vt_ref.dtype)
    for hd in range(n_heads):
        sl = slice(hd * HEAD_DIM, (hd + 1) * HEAD_DIM)
        qh = _rms_norm(qkv[:, hd * HEAD_DIM:(hd + 1) * HEAD_DIM], qg_ref[...])
        qh = (qh * cos + pltpu.roll(qh, HEAD_DIM // 2, axis=1) * sin_signed) * Q_SCALE
        q_ref[:, sl] = qh.astype(q_ref.dtype)
        kh = _rms_norm(qkv[:, d_model + hd * HEAD_DIM:d_model + (hd + 1) * HEAD_DIM], kg_ref[...])
        kh = kh * cos + pltpu.roll(kh, HEAD_DIM // 2, axis=1) * sin_signed
        k_ref[:, sl] = kh.astype(k_ref.dtype)
        for j in range(tile // MB_BLOCK):
            km_ref[j, :, sl] = jnp.mean(kh[j * MB_BLOCK:(j + 1) * MB_BLOCK, :], axis=0, keepdims=True)


def _moba_qkv(x, gain, w_qkv, q_gain, k_gain, cos, sin_signed, *, tile):
    bsz, seq, d_model = x.shape
    n_heads = d_model // HEAD_DIM
    tok = pl.BlockSpec((None, tile, d_model), lambda b, t: (b, t, 0))
    rope = pl.BlockSpec((tile, HEAD_DIM), lambda b, t: (t, 0))
    vec = _resident((1, HEAD_DIM))
    act = jax.ShapeDtypeStruct((bsz, seq, d_model), BF16)
    return pl.pallas_call(
        functools.partial(_qkv_kernel, n_heads=n_heads),
        grid=(bsz, seq // tile),
        in_specs=[
            tok,
            _resident((1, d_model)),
            _resident((d_model, 3 * d_model)),
            vec, vec, rope, rope,
        ],
        out_specs=[
            tok, tok,
            pl.BlockSpec((None, tile // MB_BLOCK, d_model, MB_BLOCK), lambda b, t: (b, t, 0, 0)),
            pl.BlockSpec((None, tile // MB_BLOCK, 1, d_model), lambda b, t: (b, t, 0, 0)),
        ],
        out_shape=[act, act,
                   jax.ShapeDtypeStruct((bsz, seq // MB_BLOCK, d_model, MB_BLOCK), BF16),
                   jax.ShapeDtypeStruct((bsz, seq // MB_BLOCK, 1, d_model), F32)],
        compiler_params=_params(2),
        name="moba_qkv",
    )(x, gain.reshape(1, d_model), w_qkv, q_gain.reshape(1, HEAD_DIM), k_gain.reshape(1, HEAD_DIM),
      cos, sin_signed)


def _moba_attn_kernel(q_ref, k_ref, vt_ref, km_ref, o_ref, sel_ref, m_ref, l_ref, acc_ref, *,
                      heads_per_step):
    qb = pl.program_id(2)
    n_blk = km_ref.shape[1]
    neg_inf = -jnp.inf
    heads = range(heads_per_step)
    hs = [slice(hd * HEAD_DIM, (hd + 1) * HEAD_DIM) for hd in heads]
    qs = [q_ref[:, sl] for sl in hs]

    blk_i = lax.broadcasted_iota(jnp.int32, (n_blk, MB_BLOCK), 0)
    for hd in heads:
        km = km_ref[hd]
        km_hi = km.astype(BF16)
        km_lo = (km - km_hi.astype(F32)).astype(BF16)
        gate = _dot_nt(km_hi, qs[hd]) + _dot_nt(km_lo, qs[hd])
        gate = jnp.where(blk_i < qb, gate, neg_inf)
        sel = jnp.zeros(gate.shape, F32)
        for _ in range(MB_TOPK):
            top = jnp.max(gate, axis=0, keepdims=True)
            idx = jnp.min(jnp.where(gate == top, blk_i, n_blk), axis=0, keepdims=True)
            pick = (blk_i == idx) & (top > neg_inf)
            sel = jnp.where(pick, 1.0, sel)
            gate = jnp.where(pick, neg_inf, gate)
        sel_ref[hd] = sel

    own = pl.ds(pl.multiple_of(qb * MB_BLOCK, MB_BLOCK), MB_BLOCK)
    key_i = lax.broadcasted_iota(jnp.int32, (MB_BLOCK, MB_BLOCK), 0)
    qry_i = lax.broadcasted_iota(jnp.int32, (MB_BLOCK, MB_BLOCK), 1)
    own_scores = [_dot_nt(k_ref[own, hs[hd]], qs[hd]) for hd in heads]
    for hd in heads:
        s = jnp.where(key_i <= qry_i, own_scores[hd], neg_inf)
        m = jnp.max(s, axis=0, keepdims=True)
        p = jnp.exp2(s - m)
        m_ref[hd] = m
        l_ref[hd] = jnp.sum(p, axis=0, keepdims=True)
        acc_ref[hd] = _dot(vt_ref[qb, hs[hd], :], p.astype(BF16))

    def body(j, carry):
        blk = pl.ds(pl.multiple_of(j * MB_BLOCK, MB_BLOCK), MB_BLOCK)
        scores = [_dot_nt(k_ref[blk, hs[hd]], qs[hd]) for hd in heads]
        for hd in heads:
            m = m_ref[hd]
            chosen = sel_ref[hd, pl.ds(j, 1), :] > 0.5
            s = jnp.where(chosen, scores[hd], neg_inf)
            m_new = jnp.maximum(m, jnp.max(s, axis=0, keepdims=True))
            alpha = jnp.exp2(m - m_new)
            p = jnp.exp2(s - m_new)
            m_ref[hd] = m_new
            l_ref[hd] = alpha * l_ref[hd] + jnp.sum(p, axis=0, keepdims=True)
            acc_ref[hd] = alpha * acc_ref[hd] + _dot(vt_ref[j, hs[hd], :], p.astype(BF16))
        return carry

    lax.fori_loop(0, qb, body, 0)
    for hd in heads:
        o_ref[:, hs[hd]] = (acc_ref[hd] / l_ref[hd]).T.astype(o_ref.dtype)


def _moba_attention(q, k, v_t, k_mean, *, heads_per_step):
    bsz, seq, d_model = q.shape
    n_heads = d_model // HEAD_DIM
    n_blk = seq // MB_BLOCK
    width = heads_per_step * HEAD_DIM
    q_spec = pl.BlockSpec((None, MB_BLOCK, width), lambda b, h, i: (b, i, h))
    return pl.pallas_call(
        functools.partial(_moba_attn_kernel, heads_per_step=heads_per_step),
        grid=(bsz, n_heads // heads_per_step, n_blk),
        in_specs=[
            q_spec,
            pl.BlockSpec((None, seq, width), lambda b, h, i: (b, 0, h)),
            pl.BlockSpec((None, n_blk, width, MB_BLOCK), lambda b, h, i: (b, 0, h, 0)),
            pl.BlockSpec((None, heads_per_step, n_blk, HEAD_DIM), lambda b, h, i: (b, h, 0, 0)),
        ],
        out_specs=q_spec,
        out_shape=jax.ShapeDtypeStruct((bsz, seq, d_model), BF16),
        scratch_shapes=[
            pltpu.VMEM((heads_per_step, n_blk, MB_BLOCK), F32),
            pltpu.VMEM((heads_per_step, 1, MB_BLOCK), F32),
            pltpu.VMEM((heads_per_step, 1, MB_BLOCK), F32),
            pltpu.VMEM((heads_per_step, HEAD_DIM, MB_BLOCK), F32),
        ],
        compiler_params=_params(3),
        name="moba_attention",
    )(q, k, v_t, k_mean)


def _ffn_kernel(x_ref, o_ref, wout_ref, gain_ref, wup_ref, cw_ref, cb_ref, wdn_ref,
                y_ref, carry_ref, acc_ref, *, f_chunk):
    tile, d_model = x_ref.shape
    ffn_dim = wdn_ref.shape[0]

    @pl.when(pl.program_id(1) == 0)
    def _():
        carry_ref[...] = jnp.zeros_like(carry_ref)

    x1 = x_ref[...] + _dot(o_ref[...], wout_ref[...])
    h = _rms_norm(x1, gain_ref[...]).astype(BF16)
    row = lax.broadcasted_iota(jnp.int32, (tile, f_chunk), 0)
    n_chunks = ffn_dim // f_chunk

    def up_proj(fc):
        lo = fc * f_chunk
        return _dot(h, wup_ref[:, lo:lo + f_chunk]), _dot(h, wup_ref[:, ffn_dim + lo:ffn_dim + lo + f_chunk])

    ahead = up_proj(0)
    for fc in range(n_chunks):
        cs = slice(fc * f_chunk, (fc + 1) * f_chunk)
        a, u = ahead
        if fc + 1 < n_chunks:
            ahead = up_proj(fc + 1)
        prev = carry_ref[:, cs]
        p1 = prev[CARRY_ROWS - 1:CARRY_ROWS, :]
        p2 = prev[CARRY_ROWS - 2:CARRY_ROWS - 1, :]
        a1 = jnp.where(row == 0, p1, pltpu.roll(a, 1, axis=0))
        a2 = jnp.where(row == 0, p2, jnp.where(row == 1, p1, pltpu.roll(a, 2, axis=0)))
        carry_ref[:, cs] = a[tile - CARRY_ROWS:tile, :]
        conv = cw_ref[0:1, cs] * a2 + cw_ref[1:2, cs] * a1 + cw_ref[2:3, cs] * a + cb_ref[:, cs]
        act = (conv * jax.nn.sigmoid(conv) * u).astype(BF16)
        part = _dot(act, wdn_ref[cs, :])
        if fc == 0:
            acc_ref[...] = part
        else:
            acc_ref[...] += part
    y_ref[...] = x1 + acc_ref[...]


def _outproj_ffn(x, o, w_out, gain, w_up, conv_w, conv_b, w_down, *, tile, f_chunk):
    bsz, seq, d_model = x.shape
    ffn_dim = w_down.shape[0]
    tok = pl.BlockSpec((None, tile, d_model), lambda b, t: (b, t, 0))
    full = _resident
    return pl.pallas_call(
        functools.partial(_ffn_kernel, f_chunk=f_chunk),
        grid=(bsz, seq // tile),
        in_specs=[
            tok, tok,
            full((d_model, d_model)),
            full((1, d_model)),
            full((d_model, 2 * ffn_dim)),
            full((CONV_WIDTH, ffn_dim)),
            full((1, ffn_dim)),
            full((ffn_dim, d_model)),
        ],
        out_specs=tok,
        out_shape=jax.ShapeDtypeStruct((bsz, seq, d_model), x.dtype),
        scratch_shapes=[
            pltpu.VMEM((CARRY_ROWS, ffn_dim), F32),
            pltpu.VMEM((tile, d_model), F32),
        ],
        compiler_params=_params(2),
        name="outproj_convglu",
    )(x, o, w_out, gain.reshape(1, d_model), w_up, conv_w, conv_b.reshape(1, ffn_dim), w_down)


def _rope_tables(seq):
    inv = 1.0 / (ROPE_THETA ** (jnp.arange(0, HEAD_DIM, 2, dtype=F32) / HEAD_DIM))
    ang = jnp.arange(seq, dtype=F32)[:, None] * inv[None, :]
    ang = jnp.concatenate([ang, ang], axis=-1)
    sign = jnp.where(jnp.arange(HEAD_DIM) < HEAD_DIM // 2, -1.0, 1.0).astype(F32)
    return jnp.cos(ang), jnp.sin(ang) * sign


def kernel(x, attn_norm, ffn_norm, hgrn_w_in, hgrn_lb, hgrn_out_norm, hgrn_w_out, moba_w_qkv, moba_q_norm, moba_k_norm, moba_w_out, ffn_w_up, ffn_conv_w, ffn_conv_b, ffn_w_down):
    bsz, seq, d_model = x.shape
    depth = attn_norm.shape[0]
    n_heads = d_model // HEAD_DIM
    n_blk = seq // MB_BLOCK
    assert d_model % HEAD_DIM == 0 and seq % MB_BLOCK == 0 and n_blk % 8 == 0
    tile = min(512, seq)
    cos, sin_signed = _rope_tables(seq)
    bf = lambda w: w.astype(BF16)

    for layer in range(depth):
        slot = layer // 2
        if layer % 2 == 0:
            o = _hgrn_mixer(x, attn_norm[layer], bf(hgrn_w_in[slot]), hgrn_lb, hgrn_out_norm[slot],
                            slot=slot, tile=tile)
            w_out = bf(hgrn_w_out[slot])
        else:
            q, k, v_t, k_mean = _moba_qkv(x, attn_norm[layer], bf(moba_w_qkv[slot]), moba_q_norm[slot],
                                          moba_k_norm[slot], cos, sin_signed, tile=MB_BLOCK)
            k_mean = k_mean.reshape(bsz, n_blk, n_heads, HEAD_DIM).transpose(0, 2, 1, 3)
            o = _moba_attention(q, k, v_t, k_mean, heads_per_step=n_heads)
            w_out = bf(moba_w_out[slot])
        x = _outproj_ffn(x, o, w_out, ffn_norm[layer], bf(ffn_w_up[layer]), ffn_conv_w[layer],
                         ffn_conv_b[layer], bf(ffn_w_down[layer]), tile=tile, f_chunk=1024)
    return x
```

```python
import functools

import jax
import jax.numpy as jnp
from jax import lax
from jax.experimental import pallas as pl
from jax.experimental.pallas import tpu as pltpu

EPS = 1e-6
HEAD_DIM = 128
HG_CHUNK = 64
MB_BLOCK = 256
MB_TOPK = 3
ROPE_THETA = 10000.0
CONV_WIDTH = 3
CARRY_ROWS = 8
Q_SCALE = HEAD_DIM ** -0.5 * 1.4426950408889634
VMEM_LIMIT_BYTES = 56 * 1024 * 1024

F32 = jnp.float32
BF16 = jnp.bfloat16


def _dot(a, b):
    return jnp.dot(a, b, preferred_element_type=F32)


def _dot_nt(a, b):
    return lax.dot_general(a, b, (((1,), (1,)), ((), ())), preferred_element_type=F32)


def _dot_tn(a, b):
    return lax.dot_general(a, b, (((0,), (0,)), ((), ())), preferred_element_type=F32)


def _rms_norm(x, gain):
    return x * lax.rsqrt(jnp.mean(x * x, axis=-1, keepdims=True) + EPS) * gain


def _params(n_grid_axes):
    return pltpu.CompilerParams(
        dimension_semantics=("arbitrary",) * n_grid_axes,
        vmem_limit_bytes=VMEM_LIMIT_BYTES,
    )


def _resident(shape):
    return pl.BlockSpec(shape, lambda *_: (0,) * len(shape), pipeline_mode=pl.Buffered(1))


def _resident_layer(shape, layer):
    return pl.BlockSpec((None,) + tuple(shape), lambda *_: (layer,) + (0,) * len(shape),
                        pipeline_mode=pl.Buffered(1))


def _hgrn_kernel(x_ref, gain_ref, win_ref, lb_ref, og_ref, o_ref, proj_ref, state_ref, *,
                 slot, n_heads):
    tile, d_model = x_ref.shape
    n_chunks = tile // HG_CHUNK

    @pl.when(pl.program_id(1) == 0)
    def _():
        state_ref[...] = jnp.zeros_like(state_ref)

    h = _rms_norm(x_ref[...], gain_ref[...]).astype(BF16)
    proj_ref[...] = _dot(h, win_ref[...])

    logits = lb_ref[...]
    rows = [logits[i:i + 1, :] for i in range(logits.shape[0])]
    top = functools.reduce(jnp.maximum, rows)
    exps = [jnp.exp(r - top) for r in rows]
    denom = functools.reduce(jnp.add, exps)
    cum = []
    for e in exps:
        p = e / denom
        cum.append(p if not cum else cum[-1] + p)
    lb = cum[slot] - cum[0]
    one_m_lb = 1.0 - lb

    r_i = lax.broadcasted_iota(jnp.int32, (HG_CHUNK, HG_CHUNK), 0)
    c_i = lax.broadcasted_iota(jnp.int32, (HG_CHUNK, HG_CHUNK), 1)
    causal = r_i >= c_i
    tril = causal.astype(BF16)
    og = og_ref[...]
    mid = HG_CHUNK // 2 - 1

    def chunk_body(c, carry):
        rows_c = pl.ds(pl.multiple_of(c * HG_CHUNK, HG_CHUNK), HG_CHUNK)
        q = proj_ref[rows_c, 0:d_model] * (HEAD_DIM ** -0.5)
        fz = proj_ref[rows_c, d_model:2 * d_model]
        v = proj_ref[rows_c, 2 * d_model:3 * d_model].astype(BF16)
        g = proj_ref[rows_c, 3 * d_model:4 * d_model]

        e_neg = jnp.exp(-jnp.abs(fz))
        inv = 1.0 / (1.0 + e_neg)
        small = e_neg * inv
        pos = fz >= 0.0
        log_f = jnp.log(lb + one_m_lb * jnp.where(pos, inv, small))
        k = one_m_lb * jnp.where(pos, small, inv)

        hi = log_f.astype(BF16)
        lo = (log_f - hi.astype(F32)).astype(BF16)
        G = _dot(tril, hi) + _dot(tril, lo)
        g_mid = G[mid:mid + 1, :]
        g_last = G[HG_CHUNK - 1:HG_CHUNK, :]
        q_rel = (q * jnp.exp(G - g_mid)).astype(BF16)
        k_rel = (k * jnp.exp(g_mid - G)).astype(BF16)
        q_in = (q * jnp.exp(G)).astype(BF16)
        k_st = (k * jnp.exp(g_last - G)).astype(BF16)
        decay = jnp.exp(g_last)
        gate = g * jax.nn.sigmoid(g)

        hs = [slice(hd * HEAD_DIM, (hd + 1) * HEAD_DIM) for hd in range(n_heads)]
        att_raw = [_dot_nt(q_rel[:, sl], k_rel[:, sl]) for sl in hs]
        o_inter = []
        for hd, sl in enumerate(hs):
            st = state_ref[hd]
            o_inter.append(_dot_nt(q_in[:, sl], st.astype(BF16)))
            state_ref[hd] = st * decay[:, sl] + _dot_tn(v[:, sl], k_st[:, sl])
        for hd, sl in enumerate(hs):
            att = jnp.where(causal, att_raw[hd], 0.0).astype(BF16)
            o = _dot(att, v[:, sl]) + o_inter[hd]
            y = _rms_norm(o, og) * gate[:, sl]
            o_ref[rows_c, sl] = y.astype(o_ref.dtype)
        return carry

    lax.fori_loop(0, n_chunks, chunk_body, 0, unroll=2)


def _hgrn_mixer(x, gain, w_in, lb_logits, out_gain, *, slot, tile):
    bsz, seq, d_model = x.shape
    n_heads = d_model // HEAD_DIM
    n_layers = lb_logits.shape[0]
    kern = functools.partial(_hgrn_kernel, slot=slot, n_heads=n_heads)
    return pl.pallas_call(
        kern,
        grid=(bsz, seq // tile),
        in_specs=[
            pl.BlockSpec((None, tile, d_model), lambda b, t: (b, t, 0)),
            _resident((1, d_model)),
            _resident_layer((d_model, 4 * d_model), slot),
            _resident((n_layers, d_model)),
            _resident((1, HEAD_DIM)),
        ],
        out_specs=pl.BlockSpec((None, tile, d_model), lambda b, t: (b, t, 0)),
        out_shape=jax.ShapeDtypeStruct((bsz, seq, d_model), BF16),
        scratch_shapes=[
            pltpu.VMEM((tile, 4 * d_model), F32),
            pltpu.VMEM((n_heads, HEAD_DIM, HEAD_DIM), F32),
        ],
        compiler_params=_params(2),
        name="hgrn_mixer",
    )(x, gain.reshape(1, d_model), w_in, lb_logits, out_gain.reshape(1, HEAD_DIM))


def _qkv_kernel(x_ref, gain_ref, w_ref, qg_ref, kg_ref, cos_ref, sin_ref,
                q_ref, k_ref, vt_ref, km_ref, *, n_heads):
    tile, d_model = x_ref.shape
    h = _rms_norm(x_ref[...], gain_ref[...]).astype(BF16)
    qkv = _dot(h, w_ref[...])
    cos = cos_ref[...]
    sin_signed = sin_ref[...]
    for j in range(tile // MB_BLOCK):
        vt_ref[j] = qkv[j * MB_BLOCK:(j + 1) * MB_BLOCK, 2 * d_model:3 * d_model].T.astype(vt_ref.dtype)
    for hd in range(n_heads):
        sl = slice(hd * HEAD_DIM, (hd + 1) * HEAD_DIM)
        qh = _rms_norm(qkv[:, hd * HEAD_DIM:(hd + 1) * HEAD_DIM], qg_ref[...])
        qh = (qh * cos + pltpu.roll(qh, HEAD_DIM // 2, axis=1) * sin_signed) * Q_SCALE
        q_ref[:, sl] = qh.astype(q_ref.dtype)
        kh = _rms_norm(qkv[:, d_model + hd * HEAD_DIM:d_model + (hd + 1) * HEAD_DIM], kg_ref[...])
        kh = kh * cos + pltpu.roll(kh, HEAD_DIM // 2, axis=1) * sin_signed
        k_ref[:, sl] = kh.astype(k_ref.dtype)
        for j in range(tile // MB_BLOCK):
            km_ref[j, :, sl] = jnp.mean(kh[j * MB_BLOCK:(j + 1) * MB_BLOCK, :], axis=0, keepdims=True)


def _moba_qkv(x, gain, w_qkv, q_gain, k_gain, cos, sin_signed, *, slot, tile):
    bsz, seq, d_model = x.shape
    n_heads = d_model // HEAD_DIM
    tok = pl.BlockSpec((None, tile, d_model), lambda b, t: (b, t, 0))
    rope = pl.BlockSpec((tile, HEAD_DIM), lambda b, t: (t, 0))
    vec = _resident((1, HEAD_DIM))
    act = jax.ShapeDtypeStruct((bsz, seq, d_model), BF16)
    return pl.pallas_call(
        functools.partial(_qkv_kernel, n_heads=n_heads),
        grid=(bsz, seq // tile),
        in_specs=[
            tok,
            _resident((1, d_model)),
            _resident_layer((d_model, 3 * d_model), slot),
            vec, vec, rope, rope,
        ],
        out_specs=[
            tok, tok,
            pl.BlockSpec((None, tile // MB_BLOCK, d_model, MB_BLOCK), lambda b, t: (b, t, 0, 0)),
            pl.BlockSpec((None, tile // MB_BLOCK, 1, d_model), lambda b, t: (b, t, 0, 0)),
        ],
        out_shape=[act, act,
                   jax.ShapeDtypeStruct((bsz, seq // MB_BLOCK, d_model, MB_BLOCK), BF16),
                   jax.ShapeDtypeStruct((bsz, seq // MB_BLOCK, 1, d_model), F32)],
        compiler_params=_params(2),
        name="moba_qkv",
    )(x, gain.reshape(1, d_model), w_qkv, q_gain.reshape(1, HEAD_DIM), k_gain.reshape(1, HEAD_DIM),
      cos, sin_signed)


def _moba_attn_kernel(q_ref, k_ref, vt_ref, km_ref, o_ref, sel_ref, m_ref, l_ref, acc_ref, *,
                      heads_per_step):
    qb = pl.program_id(2)
    n_blk = km_ref.shape[1]
    neg_inf = -jnp.inf
    heads = range(heads_per_step)
    hs = [slice(hd * HEAD_DIM, (hd + 1) * HEAD_DIM) for hd in heads]
    qs = [q_ref[:, sl] for sl in hs]

    blk_i = lax.broadcasted_iota(jnp.int32, (n_blk, MB_BLOCK), 0)
    for hd in heads:
        km = km_ref[hd]
        km_hi = km.astype(BF16)
        km_lo = (km - km_hi.astype(F32)).astype(BF16)
        gate = _dot_nt(km_hi, qs[hd]) + _dot_nt(km_lo, qs[hd])
        gate = jnp.where(blk_i < qb, gate, neg_inf)
        sel = jnp.zeros(gate.shape, F32)
        for _ in range(MB_TOPK):
            top = jnp.max(gate, axis=0, keepdims=True)
            idx = jnp.min(jnp.where(gate == top, blk_i, n_blk), axis=0, keepdims=True)
            pick = (blk_i == idx) & (top > neg_inf)
            sel = jnp.where(pick, 1.0, sel)
            gate = jnp.where(pick, neg_inf, gate)
        sel_ref[hd] = sel

    own = pl.ds(pl.multiple_of(qb * MB_BLOCK, MB_BLOCK), MB_BLOCK)
    key_i = lax.broadcasted_iota(jnp.int32, (MB_BLOCK, MB_BLOCK), 0)
    qry_i = lax.broadcasted_iota(jnp.int32, (MB_BLOCK, MB_BLOCK), 1)
    own_scores = [_dot_nt(k_ref[own, hs[hd]], qs[hd]) for hd in heads]
    for hd in heads:
        s = jnp.where(key_i <= qry_i, own_scores[hd], neg_inf)
        m = jnp.max(s, axis=0, keepdims=True)
        p = jnp.exp2(s - m)
        m_ref[hd] = m
        l_ref[hd] = jnp.sum(p, axis=0, keepdims=True)
        acc_ref[hd] = _dot(vt_ref[qb, hs[hd], :], p.astype(BF16))

    def body(j, carry):
        blk = pl.ds(pl.multiple_of(j * MB_BLOCK, MB_BLOCK), MB_BLOCK)
        scores = [_dot_nt(k_ref[blk, hs[hd]], qs[hd]) for hd in heads]
        for hd in heads:
            m = m_ref[hd]
            chosen = sel_ref[hd, pl.ds(j, 1), :] > 0.5
            s = jnp.where(chosen, scores[hd], neg_inf)
            m_new = jnp.maximum(m, jnp.max(s, axis=0, keepdims=True))
            alpha = jnp.exp2(m - m_new)
            p = jnp.exp2(s - m_new)
            m_ref[hd] = m_new
            l_ref[hd] = alpha * l_ref[hd] + jnp.sum(p, axis=0, keepdims=True)
            acc_ref[hd] = alpha * acc_ref[hd] + _dot(vt_ref[j, hs[hd], :], p.astype(BF16))
        return carry

    lax.fori_loop(0, qb, body, 0)
    for hd in heads:
        o_ref[:, hs[hd]] = (acc_ref[hd] / l_ref[hd]).T.astype(o_ref.dtype)


def _moba_attention(q, k, v_t, k_mean, *, heads_per_step):
    bsz, seq, d_model = q.shape
    n_heads = d_model // HEAD_DIM
    n_blk = seq // MB_BLOCK
    width = heads_per_step * HEAD_DIM
    q_spec = pl.BlockSpec((None, MB_BLOCK, width), lambda b, h, i: (b, i, h))
    return pl.pallas_call(
        functools.partial(_moba_attn_kernel, heads_per_step=heads_per_step),
        grid=(bsz, n_heads // heads_per_step, n_blk),
        in_specs=[
            q_spec,
            pl.BlockSpec((None, seq, width), lambda b, h, i: (b, 0, h)),
            pl.BlockSpec((None, n_blk, width, MB_BLOCK), lambda b, h, i: (b, 0, h, 0)),
            pl.BlockSpec((None, heads_per_step, n_blk, HEAD_DIM), lambda b, h, i: (b, h, 0, 0)),
        ],
        out_specs=q_spec,
        out_shape=jax.ShapeDtypeStruct((bsz, seq, d_model), BF16),
        scratch_shapes=[
            pltpu.VMEM((heads_per_step, n_blk, MB_BLOCK), F32),
            pltpu.VMEM((heads_per_step, 1, MB_BLOCK), F32),
            pltpu.VMEM((heads_per_step, 1, MB_BLOCK), F32),
            pltpu.VMEM((heads_per_step, HEAD_DIM, MB_BLOCK), F32),
        ],
        compiler_params=_params(3),
        name="moba_attention",
    )(q, k, v_t, k_mean)


def _ffn_kernel(x_ref, o_ref, wout_ref, gain_ref, wup_ref, cw_ref, cb_ref, wdn_ref,
                y_ref, carry_ref, acc_ref, *, f_chunk):
    tile, d_model = x_ref.shape
    ffn_dim = wdn_ref.shape[0]

    @pl.when(pl.program_id(1) == 0)
    def _():
        carry_ref[...] = jnp.zeros_like(carry_ref)

    x1 = x_ref[...] + _dot(o_ref[...], wout_ref[...])
    h = _rms_norm(x1, gain_ref[...]).astype(BF16)
    row = lax.broadcasted_iota(jnp.int32, (tile, f_chunk), 0)
    n_chunks = ffn_dim // f_chunk

    def up_proj(fc):
        lo = fc * f_chunk
        return _dot(h, wup_ref[:, lo:lo + f_chunk]), _dot(h, wup_ref[:, ffn_dim + lo:ffn_dim + lo + f_chunk])

    ahead = up_proj(0)
    for fc in range(n_chunks):
        cs = slice(fc * f_chunk, (fc + 1) * f_chunk)
        a, u = ahead
        if fc + 1 < n_chunks:
            ahead = up_proj(fc + 1)
        prev = carry_ref[:, cs]
        p1 = prev[CARRY_ROWS - 1:CARRY_ROWS, :]
        p2 = prev[CARRY_ROWS - 2:CARRY_ROWS - 1, :]
        a1 = jnp.where(row == 0, p1, pltpu.roll(a, 1, axis=0))
        a2 = jnp.where(row == 0, p2, jnp.where(row == 1, p1, pltpu.roll(a, 2, axis=0)))
        carry_ref[:, cs] = a[tile - CARRY_ROWS:tile, :]
        conv = cw_ref[0:1, cs] * a2 + cw_ref[1:2, cs] * a1 + cw_ref[2:3, cs] * a + cb_ref[:, cs]
        act = (conv * jax.nn.sigmoid(conv) * u).astype(BF16)
        part = _dot(act, wdn_ref[cs, :])
        if fc == 0:
            acc_ref[...] = part
        else:
            acc_ref[...] += part
    y_ref[...] = x1 + acc_ref[...]


def _outproj_ffn(x, o, w_out, gain, w_up, conv_w, conv_b, w_down, *, slot, layer, tile, f_chunk):
    bsz, seq, d_model = x.shape
    ffn_dim = w_down.shape[1]
    tok = pl.BlockSpec((None, tile, d_model), lambda b, t: (b, t, 0))
    return pl.pallas_call(
        functools.partial(_ffn_kernel, f_chunk=f_chunk),
        grid=(bsz, seq // tile),
        in_specs=[
            tok, tok,
            _resident_layer((d_model, d_model), slot),
            _resident((1, d_model)),
            _resident_layer((d_model, 2 * ffn_dim), layer),
            _resident((CONV_WIDTH, ffn_dim)),
            _resident((1, ffn_dim)),
            _resident_layer((ffn_dim, d_model), layer),
        ],
        out_specs=tok,
        out_shape=jax.ShapeDtypeStruct((bsz, seq, d_model), x.dtype),
        scratch_shapes=[
            pltpu.VMEM((CARRY_ROWS, ffn_dim), F32),
            pltpu.VMEM((tile, d_model), F32),
        ],
        compiler_params=_params(2),
        name="outproj_convglu",
    )(x, o, w_out, gain.reshape(1, d_model), w_up, conv_w, conv_b.reshape(1, ffn_dim), w_down)


def _rope_tables(seq):
    inv = 1.0 / (ROPE_THETA ** (jnp.arange(0, HEAD_DIM, 2, dtype=F32) / HEAD_DIM))
    ang = jnp.arange(seq, dtype=F32)[:, None] * inv[None, :]
    ang = jnp.concatenate([ang, ang], axis=-1)
    sign = jnp.where(jnp.arange(HEAD_DIM) < HEAD_DIM // 2, -1.0, 1.0).astype(F32)
    return jnp.cos(ang), jnp.sin(ang) * sign


def kernel(x, attn_norm, ffn_norm, hgrn_w_in, hgrn_lb, hgrn_out_norm, hgrn_w_out, moba_w_qkv, moba_q_norm, moba_k_norm, moba_w_out, ffn_w_up, ffn_conv_w, ffn_conv_b, ffn_w_down):
    bsz, seq, d_model = x.shape
    depth = attn_norm.shape[0]
    n_heads = d_model // HEAD_DIM
    n_blk = seq // MB_BLOCK
    assert d_model % HEAD_DIM == 0 and seq % MB_BLOCK == 0 and n_blk % 8 == 0
    tile = min(512, seq)
    cos, sin_signed = _rope_tables(seq)
    bf = lambda w: w.astype(BF16)
    hgrn_w_in, hgrn_w_out = bf(hgrn_w_in), bf(hgrn_w_out)
    moba_w_qkv, moba_w_out = bf(moba_w_qkv), bf(moba_w_out)
    ffn_w_up, ffn_w_down = bf(ffn_w_up), bf(ffn_w_down)

    for layer in range(depth):
        slot = layer // 2
        if layer % 2 == 0:
            o = _hgrn_mixer(x, attn_norm[layer], hgrn_w_in, hgrn_lb, hgrn_out_norm[slot],
                            slot=slot, tile=tile)
            w_out = hgrn_w_out
        else:
            q, k, v_t, k_mean = _moba_qkv(x, attn_norm[layer], moba_w_qkv, moba_q_norm[slot],
                                          moba_k_norm[slot], cos, sin_signed, slot=slot, tile=MB_BLOCK)
            k_mean = k_mean.reshape(bsz, n_blk, n_heads, HEAD_DIM).transpose(0, 2, 1, 3)
            o = _moba_attention(q, k, v_t, k_mean, heads_per_step=n_heads)
            w_out = moba_w_out
        x = _outproj_ffn(x, o, w_out, ffn_norm[layer], ffn_w_up, ffn_conv_w[layer], ffn_conv_b[layer],
                         ffn_w_down, slot=slot, layer=layer, tile=tile, f_chunk=1024)
    return x
```
